```python
import math
import jax, jax.numpy as jnp
from jax import lax
import numpy as np

D_MODEL = 1024
BATCH = 16
SEQ = 4096
DEPTH = 2
DEC_BATCH = 16
DEC_SEQ = 64
PAST_LEN = 1024

CHUNK = 64
H_A = 4
DH_A = 64
DV_A = 2 * DH_A
ROT_DIM = DH_A // 4
ROPE_THETA = 500000.0
Q_BLOCK = 128
ATTN_W = H_A * DV_A
D_INNER = D_MODEL
P_S = 64
H_S = D_INNER // P_S
G_S = 4
R_S = H_S // G_S
N_S = 64
CONV_W = 4
CONV_DIM = D_INNER + 2 * G_S * N_S
POOL_SIZES = (2, 4, 8, 16)
N_POOL = 4
POOL_GW = D_MODEL // 8
POOL_DIM = N_POOL * POOL_GW
POOL_MAX = 16
N_BRANCH = 3
Q_COLS = H_A * 2 * DH_A
K_COLS = H_A * 2 * DH_A
V_COLS = H_A * DV_A
OFF_K = Q_COLS
OFF_V = OFF_K + K_COLS
OFF_Z = OFF_V + V_COLS
OFF_XBC = OFF_Z + D_INNER
OFF_DT = OFF_XBC + CONV_DIM
OFF_POOL = OFF_DT + H_S
OFF_GATE = OFF_POOL + POOL_DIM
N_IN = OFF_GATE + N_BRANCH * D_MODEL
E_GROUPS = 4
E_PER_GROUP = 4
N_EXPERTS = E_GROUPS * E_PER_GROUP
TOP_K = 2
D_EXPERT = D_MODEL // 2
ALPHA = (2 * DEPTH) ** 0.25
BETA = (8 * DEPTH) ** -0.25
LN_EPS = 1e-5
RMS_EPS = 1e-5

kernel_name = 'hybrid_diffattn_ssd_pool_hmoe_stream'


def _layer_norm(x, g, b):
    xf = x.astype(jnp.float32)
    mu = jnp.mean(xf, axis=-1, keepdims=True)
    xc = xf - mu
    var = jnp.mean(xc * xc, axis=-1, keepdims=True)
    y = xc * lax.rsqrt(var + LN_EPS) * g.astype(jnp.float32) + b.astype(jnp.float32)
    return y.astype(x.dtype)


def _rms(x, w):
    return x * lax.rsqrt(jnp.mean(x * x, axis=-1, keepdims=True) + RMS_EPS) * w


def _rope(t, pos):
    half = ROT_DIM // 2
    inv = 1.0 / (ROPE_THETA ** (jnp.arange(half, dtype=jnp.float32) / half))
    ang = pos.astype(jnp.float32)[:, None] * inv[None, :]
    cos = jnp.cos(ang)[:, None, None, :]
    sin = jnp.sin(ang)[:, None, None, :]
    tr = t[..., :ROT_DIM].astype(jnp.float32)
    t1, t2 = tr[..., :half], tr[..., half:]
    rot = jnp.concatenate([t1 * cos - t2 * sin, t2 * cos + t1 * sin], axis=-1)
    return jnp.concatenate([rot.astype(t.dtype), t[..., ROT_DIM:]], axis=-1)


def _diff_attn(q, k, v, q_pos, k_pos, lam):
    s = jnp.einsum('bqhcd,bkhcd->bhcqk', q, k) * (DH_A ** -0.5)
    vis = (k_pos[None, :] // CHUNK) <= (q_pos[:, None] // CHUNK)
    s = jnp.where(vis, s, -jnp.inf)
    pr = jax.nn.softmax(s, axis=-1)
    w = pr[:, :, 0] - lam * pr[:, :, 1]
    return jnp.einsum('bhqk,bkhv->bqhv', w, v)


def _ssd_scan(xs, dt, a, bm, cm, h0):
    Bsz, L = xs.shape[0], xs.shape[1]
    Lp = -(-L // CHUNK) * CHUNK
    nc = Lp // CHUNK

    def padl(t):
        return jnp.pad(t, [(0, 0), (0, Lp - L)] + [(0, 0)] * (t.ndim - 2))

    xs = padl(xs).reshape(Bsz, nc, CHUNK, G_S, R_S, P_S)
    dt = padl(dt).reshape(Bsz, nc, CHUNK, G_S, R_S)
    bm = padl(bm).reshape(Bsz, nc, CHUNK, G_S, N_S)
    cm = padl(cm).reshape(Bsz, nc, CHUNK, G_S, N_S)
    acs = jnp.cumsum(dt * a, axis=2)
    acs_t = jnp.moveaxis(acs, 2, -1)
    seg = acs_t[..., :, None] - acs_t[..., None, :]
    causal = jnp.tril(jnp.ones((CHUNK, CHUNK), dtype=bool))
    lmat = jnp.exp(jnp.where(causal, seg, -jnp.inf))
    xdt = xs * dt[..., None]
    cb = jnp.einsum('bcign,bcjgn->bcgij', cm, bm)
    y_diag = jnp.einsum('bcgij,bcgrij,bcjgrp->bcigrp', cb, lmat, xdt)
    decay = jnp.exp(acs[:, :, -1:] - acs)
    states = jnp.einsum('bcjgn,bcjgr,bcjgrp->bcgrpn', bm, decay, xdt)
    chunk_decay = jnp.exp(acs[:, :, -1])

    def step(h, inp):
        s_c, d_c = inp
        return h * d_c[..., None, None] + s_c, h

    h_last, h_in = lax.scan(step, h0, (jnp.moveaxis(states, 1, 0), jnp.moveaxis(chunk_decay, 1, 0)))
    h_in = jnp.moveaxis(h_in, 0, 1)
    y_off = jnp.einsum('bcign,bcgrpn,bcigr->bcigrp', cm, h_in, jnp.exp(acs))
    y = (y_diag + y_off).reshape(Bsz, Lp, G_S, R_S, P_S)[:, :L]
    return y, h_last


def _pool_mix(u, hist, pos0, pool_w, pool_scale):
    B, L, _ = u.shape
    ext = jnp.concatenate([hist.astype(u.dtype), u], axis=1)
    cs = jnp.cumsum(ext.astype(jnp.float32), axis=1)
    cs = jnp.pad(cs, ((0, 0), (1, 0), (0, 0)))
    pos = pos0 + jnp.arange(L, dtype=jnp.int32)
    means = []
    for gi, w in enumerate(POOL_SIZES):
        c0, c1 = gi * POOL_GW, (gi + 1) * POOL_GW
        win = cs[:, POOL_MAX:POOL_MAX + L, c0:c1] - cs[:, POOL_MAX - w:POOL_MAX - w + L, c0:c1]
        cnt = jnp.minimum(pos + 1, w).astype(jnp.float32)[None, :, None]
        means.append(win / cnt)
    d = jnp.concatenate(means, axis=-1) - u.astype(jnp.float32)
    o = jnp.einsum('blgc,gce->blge', d.reshape(B, L, N_POOL, POOL_GW), pool_w.astype(jnp.float32))
    o = o.reshape(B, L, POOL_DIM) * pool_scale.astype(jnp.float32)
    return o.astype(u.dtype), ext[:, -(POOL_MAX - 1):]


def _hier_moe(x, rg_w, rg_b, re_w, re_b, w_gate, w_up, w_down):
    B, L, D = x.shape
    T = B * L
    t = x.reshape(T, D)
    g_logits = (t @ rg_w + rg_b).astype(jnp.float32)
    g_prob = jax.nn.softmax(g_logits, axis=-1)
    g_sel = jnp.argmax(g_logits, axis=-1)
    g_w = jnp.max(g_prob, axis=-1, keepdims=True)
    e_logits = (t @ re_w + re_b).astype(jnp.float32).reshape(T, E_GROUPS, E_PER_GROUP)
    e_in = jnp.einsum('tge,tg->te', e_logits, jax.nn.one_hot(g_sel, E_GROUPS, dtype=jnp.float32))
    top_v, top_i = lax.top_k(e_in, TOP_K)
    top_w = jax.nn.softmax(top_v, axis=-1) * g_w
    e_idx = g_sel[:, None] * E_PER_GROUP + top_i
    combine = jnp.sum(jax.nn.one_hot(e_idx, N_EXPERTS, dtype=jnp.float32) * top_w[..., None], axis=1)
    out = jnp.zeros((T, D), jnp.float32)
    for gi in range(E_GROUPS):
        e0, e1 = gi * E_PER_GROUP, (gi + 1) * E_PER_GROUP
        hg = jnp.einsum('td,edf->tef', t, w_gate[e0:e1]).astype(jnp.float32)
        hu = jnp.einsum('td,edf->tef', t, w_up[e0:e1]).astype(jnp.float32)
        h = jax.nn.silu(hg) * hu * combine[:, e0:e1, None]
        out = out + jnp.einsum('tef,efd->td', h.astype(x.dtype), w_down[e0:e1]).astype(jnp.float32)
    return out.reshape(B, L, D).astype(x.dtype)


def _trunk_layer(x, pos0, past_k, past_v, ssm_h0, conv_hist, pool_hist, p, layer_idx):
    B, L, _ = x.shape
    dtype = x.dtype
    pos = pos0 + jnp.arange(L, dtype=jnp.int32)
    proj = x @ p['w_in']
    q_r, k_r, v_r, z, xbc_raw, dt_raw, u_pool, g_raw = jnp.split(
        proj, [OFF_K, OFF_V, OFF_Z, OFF_XBC, OFF_DT, OFF_POOL, OFF_GATE], axis=-1)

    q = _rope(q_r.reshape(B, L, H_A, 2, DH_A), pos)
    k = _rope(k_r.reshape(B, L, H_A, 2, DH_A), pos)
    v = v_r.reshape(B, L, H_A, DV_A)
    lam_init = 0.8 - 0.6 * math.exp(-0.3 * layer_idx)
    lv = p['lam'].astype(jnp.float32)
    lam = jnp.exp(jnp.sum(lv[0] * lv[1])) - jnp.exp(jnp.sum(lv[2] * lv[3])) + lam_init
    qf = q.astype(jnp.float32)
    if past_k is None:
        kf, vf = k.astype(jnp.float32), v.astype(jnp.float32)
        nb = L // Q_BLOCK
        qb = jnp.moveaxis(qf.reshape(B, nb, Q_BLOCK, H_A, 2, DH_A), 1, 0)
        pb = pos.reshape(nb, Q_BLOCK)
        ob = lax.map(lambda a: _diff_attn(a[0], kf, vf, a[1], pos, lam), (qb, pb))
        o = jnp.moveaxis(ob, 0, 1).reshape(B, L, H_A, DV_A)
    else:
        n_past = past_k.shape[1]
        kf = jnp.concatenate([past_k.astype(dtype).reshape(B, n_past, H_A, 2, DH_A), k], axis=1).astype(jnp.float32)
        vf = jnp.concatenate([past_v.astype(dtype), v], axis=1).astype(jnp.float32)
        k_pos = jnp.arange(n_past + L, dtype=jnp.int32)
        o = _diff_attn(qf, kf, vf, pos, k_pos, lam)
    o = _rms(o, p['attn_norm'].astype(jnp.float32)) * (1.0 - lam_init)
    attn_out = o.reshape(B, L, ATTN_W).astype(dtype)

    xbc_ext = jnp.concatenate([conv_hist.astype(dtype), xbc_raw], axis=1)
    acc = p['conv_b'].astype(jnp.float32)
    for tap in range(CONV_W):
        acc = acc + xbc_ext[:, tap:tap + L].astype(jnp.float32) * p['conv_w'][tap].astype(jnp.float32)
    xbc = jax.nn.silu(acc)
    xs, bm, cm = jnp.split(xbc, [D_INNER, D_INNER + G_S * N_S], axis=-1)
    xs = xs.reshape(B, L, G_S, R_S, P_S)
    bm = bm.reshape(B, L, G_S, N_S)
    cm = cm.reshape(B, L, G_S, N_S)
    dt = jax.nn.softplus(dt_raw.astype(jnp.float32) + p['dt_bias'].astype(jnp.float32)).reshape(B, L, G_S, R_S)
    a = -jnp.exp(p['a_log'].astype(jnp.float32)).reshape(G_S, R_S)
    h0 = ssm_h0.astype(jnp.float32).reshape(B, G_S, R_S, P_S, N_S)
    y, h_last = _ssd_scan(xs, dt, a, bm, cm, h0)
    y = y + p['d_skip'].astype(jnp.float32).reshape(G_S, R_S, 1) * xs
    y = y.reshape(B, L, D_INNER) * jax.nn.silu(z.astype(jnp.float32))
    ssd_out = _rms(y, p['ssd_norm'].astype(jnp.float32)).astype(dtype)
    conv_new = xbc_ext[:, -(CONV_W - 1):]
    ssm_new = h_last.reshape(B, H_S, P_S, N_S).astype(dtype)

    pool_out, pool_new = _pool_mix(u_pool, pool_hist, pos0, p['pool_w'], p['pool_scale'])

    gates = jax.nn.sigmoid(g_raw.astype(jnp.float32)).reshape(B, L, N_BRANCH, D_MODEL)
    merged = (gates[:, :, 0] * (attn_out @ p['w_br_attn']).astype(jnp.float32)
              + gates[:, :, 1] * (ssd_out @ p['w_br_ssd']).astype(jnp.float32)
              + gates[:, :, 2] * (pool_out @ p['w_br_pool']).astype(jnp.float32))
    mix = merged.astype(dtype) @ p['w_o']
    x1 = _layer_norm(ALPHA * x + mix, p['ln1_g'], p['ln1_b'])

    ff = _hier_moe(x1, p['rg_w'], p['rg_b'], p['re_w'], p['re_b'], p['w_gate'], p['w_up'], p['w_down'])
    x2 = _layer_norm(ALPHA * x1 + ff, p['ln2_g'], p['ln2_b'])
    return x2, (k.reshape(B, L, H_A, 2 * DH_A), v, ssm_new, conv_new, pool_new)


def setup_inputs(seed: int = 0) -> dict:
    key = jax.random.key(seed)
    ks = iter(jax.random.split(key, 48))
    f32 = jnp.float32

    def nrm(shape, scale):
        return jax.random.normal(next(ks), shape, f32) * scale

    x_prompt = nrm((BATCH, SEQ, D_MODEL), 1.0)
    x_sample = nrm((DEC_BATCH, DEC_SEQ, D_MODEL), 1.0)
    cache_k = nrm((DEPTH, DEC_BATCH, PAST_LEN, H_A, 2 * DH_A), 1.0)
    cache_v = nrm((DEPTH, DEC_BATCH, PAST_LEN, H_A, DV_A), 1.0)
    state_ssm = nrm((DEPTH, DEC_BATCH, H_S, P_S, N_S), 0.1)
    state_conv = nrm((DEPTH, DEC_BATCH, CONV_W - 1, CONV_DIM), 1.0)
    state_pool = nrm((DEPTH, DEC_BATCH, POOL_MAX - 1, POOL_DIM), 1.0)
    w_in = nrm((DEPTH, D_MODEL, N_IN), D_MODEL ** -0.5)
    conv_w = nrm((DEPTH, CONV_W, CONV_DIM), CONV_W ** -0.5)
    conv_b = nrm((DEPTH, CONV_DIM), 0.02)
    dt0 = jnp.exp(jax.random.uniform(next(ks), (DEPTH, H_S), f32, math.log(1e-3), math.log(1e-1)))
    dt_bias = dt0 + jnp.log(-jnp.expm1(-dt0))
    a_log = jnp.log(jax.random.uniform(next(ks), (DEPTH, H_S), f32, 1.0, 16.0))
    d_skip = 1.0 + nrm((DEPTH, H_S), 0.1)
    ssd_norm = 1.0 + nrm((DEPTH, D_INNER), 0.05)
    lam = nrm((DEPTH, 4, DH_A), 0.1)
    attn_norm = 1.0 + nrm((DEPTH, DV_A), 0.05)
    pool_w = nrm((DEPTH, N_POOL, POOL_GW, POOL_GW), POOL_GW ** -0.5)
    pool_scale = 1.0 + nrm((DEPTH, POOL_DIM), 0.1)
    w_br_attn = nrm((DEPTH, ATTN_W, D_MODEL), ATTN_W ** -0.5)
    w_br_ssd = nrm((DEPTH, D_INNER, D_MODEL), D_INNER ** -0.5)
    w_br_pool = nrm((DEPTH, POOL_DIM, D_MODEL), POOL_DIM ** -0.5)
    w_o = nrm((DEPTH, D_MODEL, D_MODEL), BETA * D_MODEL ** -0.5)
    ln1_g = 1.0 + nrm((DEPTH, D_MODEL), 0.05)
    ln1_b = nrm((DEPTH, D_MODEL), 0.02)
    router_group_w = nrm((DEPTH, D_MODEL, E_GROUPS), D_MODEL ** -0.5)
    router_group_b = nrm((DEPTH, E_GROUPS), 0.01)
    router_expert_w = nrm((DEPTH, D_MODEL, N_EXPERTS), D_MODEL ** -0.5)
    router_expert_b = nrm((DEPTH, N_EXPERTS), 0.01)
    w_gate = nrm((DEPTH, N_EXPERTS, D_MODEL, D_EXPERT), D_MODEL ** -0.5)
    w_up = nrm((DEPTH, N_EXPERTS, D_MODEL, D_EXPERT), D_MODEL ** -0.5)
    w_down = nrm((DEPTH, N_EXPERTS, D_EXPERT, D_MODEL), BETA * D_EXPERT ** -0.5)
    ln2_g = 1.0 + nrm((DEPTH, D_MODEL), 0.05)
    ln2_b = nrm((DEPTH, D_MODEL), 0.02)
    return {
        'x_prompt': x_prompt, 'x_sample': x_sample,
        'cache_k': cache_k, 'cache_v': cache_v, 'state_ssm': state_ssm,
        'state_conv': state_conv, 'state_pool': state_pool,
        'w_in': w_in, 'conv_w': conv_w, 'conv_b': conv_b, 'dt_bias': dt_bias, 'a_log': a_log,
        'd_skip': d_skip, 'ssd_norm': ssd_norm, 'lam': lam, 'attn_norm': attn_norm,
        'pool_w': pool_w, 'pool_scale': pool_scale,
        'w_br_attn': w_br_attn, 'w_br_ssd': w_br_ssd, 'w_br_pool': w_br_pool, 'w_o': w_o,
        'ln1_g': ln1_g, 'ln1_b': ln1_b,
        'router_group_w': router_group_w, 'router_group_b': router_group_b,
        'router_expert_w': router_expert_w, 'router_expert_b': router_expert_b,
        'w_gate': w_gate, 'w_up': w_up, 'w_down': w_down,
        'ln2_g': ln2_g, 'ln2_b': ln2_b,
    }


def reference(x_prompt, x_sample, cache_k, cache_v, state_ssm, state_conv, state_pool,
              w_in, conv_w, conv_b, dt_bias, a_log, d_skip, ssd_norm, lam, attn_norm,
              pool_w, pool_scale, w_br_attn, w_br_ssd, w_br_pool, w_o, ln1_g, ln1_b,
              router_group_w, router_group_b, router_expert_w, router_expert_b,
              w_gate, w_up, w_down, ln2_g, ln2_b):
    bp = x_prompt.shape[0]
    dtp = x_prompt.dtype
    zero_h = jnp.zeros((bp, H_S, P_S, N_S), dtp)
    zero_conv = jnp.zeros((bp, CONV_W - 1, CONV_DIM), dtp)
    zero_pool = jnp.zeros((bp, POOL_MAX - 1, POOL_DIM), dtp)
    pos0_sample = cache_k.shape[2]
    hp, hs = x_prompt, x_sample
    kp, vp, sp, cp, pp = [], [], [], [], []
    ks_, vs_, ss_, cs_, ps_ = [], [], [], [], []
    for l in range(DEPTH):
        p = {
            'w_in': w_in[l], 'conv_w': conv_w[l], 'conv_b': conv_b[l], 'dt_bias': dt_bias[l],
            'a_log': a_log[l], 'd_skip': d_skip[l], 'ssd_norm': ssd_norm[l], 'lam': lam[l],
            'attn_norm': attn_norm[l], 'pool_w': pool_w[l], 'pool_scale': pool_scale[l],
            'w_br_attn': w_br_attn[l], 'w_br_ssd': w_br_ssd[l], 'w_br_pool': w_br_pool[l],
            'w_o': w_o[l], 'ln1_g': ln1_g[l], 'ln1_b': ln1_b[l],
            'rg_w': router_group_w[l], 'rg_b': router_group_b[l],
            're_w': router_expert_w[l], 're_b': router_expert_b[l],
            'w_gate': w_gate[l], 'w_up': w_up[l], 'w_down': w_down[l],
            'ln2_g': ln2_g[l], 'ln2_b': ln2_b[l],
        }
        hp, st_p = _trunk_layer(hp, 0, None, None, zero_h, zero_conv, zero_pool, p, l)
        hs, st_s = _trunk_layer(hs, pos0_sample, cache_k[l], cache_v[l], state_ssm[l],
                                state_conv[l], state_pool[l], p, l)
        kp.append(st_p[0]); vp.append(st_p[1]); sp.append(st_p[2]); cp.append(st_p[3]); pp.append(st_p[4])
        ks_.append(st_s[0]); vs_.append(st_s[1]); ss_.append(st_s[2]); cs_.append(st_s[3]); ps_.append(st_s[4])
    return (hp, hs,
            jnp.stack(kp), jnp.stack(vp), jnp.stack(sp), jnp.stack(cp), jnp.stack(pp),
            jnp.stack(ks_), jnp.stack(vs_), jnp.stack(ss_), jnp.stack(cs_), jnp.stack(ps_))
```

```python
import functools
import math

import jax
import jax.numpy as jnp
from jax import lax
from jax.experimental import pallas as pl
from jax.experimental.pallas import tpu as pltpu

F32 = jnp.float32
BF16 = jnp.bfloat16
HIGHEST = lax.Precision.HIGHEST

D_MODEL = 1024
CHUNK = 64
H_A = 4
DH_A = 64
DV_A = 2 * DH_A
ROT_DIM = DH_A // 4
ROPE_THETA = 500000.0
ATTN_W = H_A * DV_A
D_INNER = D_MODEL
P_S = 64
H_S = D_INNER // P_S
G_S = 4
R_S = H_S // G_S
N_S = 64
CONV_W = 4
CONV_DIM = D_INNER + 2 * G_S * N_S
POOL_SIZES = (2, 4, 8, 16)
N_POOL = 4
POOL_GW = D_MODEL // 8
POOL_DIM = N_POOL * POOL_GW
POOL_MAX = 16
N_BRANCH = 3
Q_COLS = H_A * 2 * DH_A
OFF_K = Q_COLS
OFF_V = OFF_K + Q_COLS
OFF_Z = OFF_V + H_A * DV_A
OFF_XBC = OFF_Z + D_INNER
OFF_DT = OFF_XBC + CONV_DIM
OFF_POOL = OFF_DT + H_S
OFF_GATE = OFF_POOL + POOL_DIM
N_IN = OFF_GATE + N_BRANCH * D_MODEL
E_GROUPS = 4
E_PER_GROUP = 4
N_EXPERTS = E_GROUPS * E_PER_GROUP
D_EXPERT = D_MODEL // 2
DEPTH_ = 2
ALPHA = (2 * DEPTH_) ** 0.25
LN_EPS = 1e-5
RMS_EPS = 1e-5

LANES = 128
GROUP_W = R_S * P_S
VMEM_LIMIT = 56 * 1024 * 1024
NEG_BIG = -1e30


def _const_spec(shape):
    nd = len(shape)
    return pl.BlockSpec(shape, lambda *_: (0,) * nd)


def _params(sem):
    return pltpu.CompilerParams(dimension_semantics=sem, vmem_limit_bytes=VMEM_LIMIT)


def _dot(a, b):
    return jnp.dot(a, b, preferred_element_type=F32)


def _dot_nt(a, b):
    return lax.dot_general(a, b, (((1,), (1,)), ((), ())), preferred_element_type=F32)


def _dot_tn(a, b):
    return lax.dot_general(a, b, (((0,), (0,)), ((), ())), preferred_element_type=F32)


def _silu(x):
    return x * jax.nn.sigmoid(x)


def _layer_norm(x, g, b):
    mu = jnp.mean(x, axis=-1, keepdims=True)
    xc = x - mu
    var = jnp.mean(xc * xc, axis=-1, keepdims=True)
    return xc * lax.rsqrt(var + LN_EPS) * g + b


def _qkv_kernel(x_ref, w_ref, cos_ref, sa_ref, sb_ref, k_ref, v_ref, qb_ref, kb_ref, vb_ref):
    xb = x_ref[...].astype(BF16)
    qkv = _dot(xb, w_ref[...])
    c, sa, sb = cos_ref[...], sa_ref[...], sb_ref[...]

    def rope(t):
        heads = []
        for h in range(H_A):
            th = t[:, h * LANES:(h + 1) * LANES]
            heads.append(th * c + pltpu.roll(th, LANES - ROT_DIM // 2, 1) * sa + pltpu.roll(th, ROT_DIM // 2, 1) * sb)
        return jnp.concatenate(heads, axis=1)

    q = rope(qkv[:, 0:OFF_K])
    k = rope(qkv[:, OFF_K:OFF_V])
    v = qkv[:, OFF_V:OFF_Z]
    k_ref[...] = k
    v_ref[...] = v
    qb_ref[...] = (q * (DH_A ** -0.5)).astype(BF16)
    kb_ref[...] = k.astype(BF16)
    vb_ref[...] = v.astype(BF16)


def _rope_tables(pos0, L):
    half = ROT_DIM // 2
    inv = 1.0 / (ROPE_THETA ** (jnp.arange(half, dtype=F32) / half))
    pos = pos0 + jnp.arange(L, dtype=jnp.int32)
    ang = pos.astype(F32)[:, None] * inv[None, :]
    cos, sin = jnp.cos(ang), jnp.sin(ang)
    rest = DH_A - ROT_DIM
    c64 = jnp.concatenate([cos, cos, jnp.ones((L, rest), F32)], axis=1)
    sa64 = jnp.concatenate([-sin, jnp.zeros((L, DH_A - half), F32)], axis=1)
    sb64 = jnp.concatenate([jnp.zeros((L, half), F32), sin, jnp.zeros((L, rest), F32)], axis=1)
    two = lambda t: jnp.concatenate([t, t], axis=1)
    return two(c64), two(sa64), two(sb64)


def _qkv_call(x, w_qkv, pos0, tm, interpret):
    B, L, D = x.shape
    cos_t, sa_t, sb_t = _rope_tables(pos0, L)
    tok = lambda w: pl.BlockSpec((None, tm, w), lambda b, i: (b, i, 0))
    tab = pl.BlockSpec((tm, LANES), lambda b, i: (i, 0))
    sds = jax.ShapeDtypeStruct
    return pl.pallas_call(
        _qkv_kernel,
        grid=(B, L // tm),
        in_specs=[tok(D), _const_spec((D, OFF_Z)), tab, tab, tab],
        out_specs=[tok(Q_COLS)] * 5,
        out_shape=[sds((B, L, Q_COLS), F32)] * 2 + [sds((B, L, Q_COLS), BF16)] * 3,
        compiler_params=_params(("parallel", "parallel")),
        interpret=interpret,
        name="qkv_rope",
    )(x, w_qkv, cos_t, sa_t, sb_t)


def _attn_kernel(q_ref, k_ref, v_ref, lam_ref, nrm_ref, o_ref, *, tq, tk, tail, n_full, causal, lam_init):
    qi = pl.program_id(2)
    q = q_ref[...]
    lane = lax.broadcasted_iota(jnp.int32, (tq, LANES), 1)
    zero = jnp.zeros_like(q)
    q2 = jnp.concatenate([jnp.where(lane < DH_A, q, zero), jnp.where(lane >= DH_A, q, zero)], axis=0)

    def update(carry, kblk, vblk, mask):
        m, l, acc = carry
        s = _dot_nt(q2, kblk)
        if mask is not None:
            s = jnp.where(mask, s, NEG_BIG)
        m_new = jnp.maximum(m, jnp.max(s, axis=-1, keepdims=True))
        alpha = jnp.exp(m - m_new)
        p = jnp.exp(s - m_new)
        l = alpha * l + jnp.sum(p, axis=-1, keepdims=True)
        acc = alpha * acc + _dot(p.astype(BF16), vblk)
        return m_new, l, acc

    def full_step(j, carry):
        r0 = pl.multiple_of(j * tk, tk)
        return update(carry, k_ref[pl.ds(r0, tk), :], v_ref[pl.ds(r0, tk), :], None)

    carry = (jnp.full((2 * tq, 1), NEG_BIG, F32), jnp.zeros((2 * tq, 1), F32), jnp.zeros((2 * tq, LANES), F32))
    if causal:
        carry = lax.fori_loop(0, qi, full_step, carry)
        t0 = pl.multiple_of(qi * tk, tk)
        row = lax.broadcasted_iota(jnp.int32, (2 * tq, tail), 0) & (tq - 1)
        col = lax.broadcasted_iota(jnp.int32, (2 * tq, tail), 1)
        shift = CHUNK.bit_length() - 1
        mask = lax.shift_right_logical(col, shift) <= lax.shift_right_logical(row, shift)
    else:
        carry = lax.fori_loop(0, n_full, full_step, carry)
        t0 = n_full * tk
        mask = None
    m, l, acc = update(carry, k_ref[pl.ds(t0, tail), :], v_ref[pl.ds(t0, tail), :], mask)

    lv = lam_ref[...]
    lam = (jnp.exp(jnp.sum(lv[0:1] * lv[1:2], axis=-1, keepdims=True))
           - jnp.exp(jnp.sum(lv[2:3] * lv[3:4], axis=-1, keepdims=True)) + lam_init)
    on = acc / l
    o = on[0:tq] - lam * on[tq:2 * tq]
    o = o * lax.rsqrt(jnp.mean(o * o, axis=-1, keepdims=True) + RMS_EPS) * nrm_ref[...] * (1.0 - lam_init)
    o_ref[...] = o.astype(BF16)


def _attn_call(qb, kb, vb, lam, attn_norm, lam_init, causal, interpret):
    B, L, _ = qb.shape
    Lk = kb.shape[1]
    if causal:
        tq = tk = tail = min(256, L)
        n_full = 0
        assert L % tq == 0 and tq % CHUNK == 0 and Lk == L
    else:
        tq, tail = L, L
        n_past = Lk - L
        tk = min(256, n_past)
        assert n_past % tk == 0
        n_full = n_past // tk
    kern = functools.partial(_attn_kernel, tq=tq, tk=tk, tail=tail, n_full=n_full, causal=causal, lam_init=lam_init)
    kv_spec = pl.BlockSpec((None, Lk, LANES), lambda b, h, i: (b, 0, h))
    q_spec = pl.BlockSpec((None, tq, LANES), lambda b, h, i: (b, i, h))
    return pl.pallas_call(
        kern,
        grid=(B, H_A, L // tq),
        in_specs=[q_spec, kv_spec, kv_spec, _const_spec((4, DH_A)), _const_spec((1, DV_A))],
        out_specs=q_spec,
        out_shape=jax.ShapeDtypeStruct((B, L, ATTN_W), BF16),
        compiler_params=_params(("parallel", "parallel", "arbitrary")),
        interpret=interpret,
        name="diff_attn",
    )(qb, kb, vb, lam, attn_norm.reshape(1, DV_A))


def _ssd_kernel(x_ref, wz_ref, wxbc_ref, wdt_ref, convw_ref, convb_ref, dtb_ref, alog_ref, dskip_ref, nrm_ref,
                expand_ref, h0_ref, chist_ref,
                y_ref, hout_ref, cnew_ref,
                ext_s, xbc_s, z_s, dt_s, h_s, *, ts, q):
    i = pl.program_id(1)
    hist0 = 8 - (CONV_W - 1)

    @pl.when(i == 0)
    def _():
        h_s[...] = jnp.zeros_like(h_s)
        for g in range(G_S):
            h_s[g, g * N_S:(g + 1) * N_S, :] = h0_ref[g]
        ext_s[hist0:8, :] = chist_ref[...]

    xb = x_ref[...].astype(BF16)
    z_s[...] = _dot(xb, wz_ref[...])
    ext_s[8:8 + ts, :] = _dot(xb, wxbc_ref[...])
    dt_raw = _dot(xb, wdt_ref[...])
    x_dt = dt_raw + dtb_ref[...]
    dt_s[...] = jnp.maximum(x_dt, 0.0) + jnp.log1p(jnp.exp(-jnp.abs(x_dt)))

    acc = convb_ref[...] + ext_s[hist0:hist0 + ts, :] * convw_ref[0:1, :]
    for tap in range(1, CONV_W):
        acc = acc + ext_s[hist0 + tap:hist0 + tap + ts, :] * convw_ref[tap:tap + 1, :]
    xbc_s[...] = _silu(acc)
    hist_new = ext_s[ts + hist0:ts + 8, :]
    cnew_ref[...] = hist_new
    ext_s[hist0:8, :] = hist_new

    a_row = -jnp.exp(alog_ref[...])
    ri = lax.broadcasted_iota(jnp.int32, (q, q), 0)
    ci = lax.broadcasted_iota(jnp.int32, (q, q), 1)
    causal = ri >= ci
    tril = jnp.where(causal, 1.0, 0.0).astype(F32)
    lane_g = lax.shift_right_logical(lax.broadcasted_iota(jnp.int32, (q, GROUP_W), 1),
                                     N_S.bit_length() - 1)

    def chunk(c, _):
        r0 = pl.multiple_of(c * q, q)
        rows = pl.ds(r0, q)
        dtc = dt_s[rows, :]
        acs = jnp.dot(tril, dtc * a_row, precision=HIGHEST, preferred_element_type=F32)
        acs_last = acs[q - 1:q, :]
        stack = jnp.concatenate([dtc, jnp.exp(acs), jnp.exp(acs_last - acs),
                                 jnp.broadcast_to(jnp.exp(acs_last), (8, LANES))], axis=0)
        ex = jnp.dot(stack, expand_ref[...], precision=HIGHEST, preferred_element_type=F32)
        dt_e, ein_e, dec_e, cd_e = ex[0:q], ex[q:2 * q], ex[2 * q:3 * q], ex[3 * q:3 * q + 1]
        xs = xbc_s[rows, 0:D_INNER]
        bm = xbc_s[rows, D_INNER:D_INNER + GROUP_W]
        cm = xbc_s[rows, D_INNER + GROUP_W:CONV_DIM]
        xdt = xs * dt_e
        xdt_b = xdt.astype(BF16)
        xdec_b = (xdt * dec_e).astype(BF16)
        bm_b = bm.astype(BF16)
        acs_t = acs.T
        zero_bc = jnp.zeros((q, GROUP_W), F32)
        zero_x = jnp.zeros((q, GROUP_W), BF16)
        ys = []
        for g in range(G_S):
            gs = slice(g * GROUP_W, (g + 1) * GROUP_W)
            cm_g = jnp.where(lane_g == g, cm, zero_bc).astype(BF16)
            bm_g = jnp.where(lane_g == g, bm, zero_bc).astype(BF16)
            cb = _dot_nt(cm_g, bm_b)
            xdt_g = xdt_b[:, gs]
            y_g = _dot(cm_g, h_s[g].astype(BF16)) * ein_e[:, gs]
            for r in range(R_S):
                h = g * R_S + r
                seg = acs[:, h:h + 1] - acs_t[h:h + 1, :]
                lm = jnp.exp(jnp.where(causal, seg, NEG_BIG))
                m_h = (cb * lm).astype(BF16)
                y_g = y_g + _dot(m_h, jnp.where(lane_g == r, xdt_g, zero_x))
            ys.append(y_g)
            h_s[g] = h_s[g] * cd_e[:, gs] + _dot_tn(bm_g, xdec_b[:, gs])
        y = jnp.concatenate(ys, axis=1) + dskip_ref[...] * xs
        y = y * _silu(z_s[rows, :])
        y = y * lax.rsqrt(jnp.mean(y * y, axis=-1, keepdims=True) + RMS_EPS) * nrm_ref[...]
        y_ref[rows, :] = y.astype(BF16)
        return 0

    lax.fori_loop(0, ts // q, chunk, 0)
    for g in range(G_S):
        hout_ref[g] = h_s[g, g * N_S:(g + 1) * N_S, :]


def _ssd_call(x, w_z, w_xbc, w_dt, conv_w, conv_b, dt_bias, a_log, d_skip, ssd_norm, h0, conv_hist, ts, q, interpret):
    B, L, D = x.shape
    pad = lambda v: jnp.pad(v.astype(F32), (0, LANES - H_S)).reshape(1, LANES)
    expand = (jnp.arange(LANES)[:, None] == (jnp.arange(D_INNER)[None, :] // P_S)).astype(F32)
    dskip_e = jnp.repeat(d_skip.astype(F32), P_S).reshape(1, D_INNER)
    h0_t = h0.astype(F32).reshape(B, G_S, R_S, P_S, N_S).transpose(0, 1, 4, 2, 3).reshape(B, G_S, N_S, GROUP_W)
    kern = functools.partial(_ssd_kernel, ts=ts, q=q)
    tok = lambda w: pl.BlockSpec((None, ts, w), lambda b, i: (b, i, 0))
    sds = jax.ShapeDtypeStruct
    y, h_t, conv_new = pl.pallas_call(
        kern,
        grid=(B, L // ts),
        in_specs=[tok(D), _const_spec((D, D_INNER)), _const_spec((D, CONV_DIM)), _const_spec((D, LANES)),
                  _const_spec((CONV_W, CONV_DIM)), _const_spec((1, CONV_DIM)), _const_spec((1, LANES)),
                  _const_spec((1, LANES)), _const_spec((1, D_INNER)), _const_spec((1, D_INNER)),
                  _const_spec((LANES, D_INNER)),
                  pl.BlockSpec((None, G_S, N_S, GROUP_W), lambda b, i: (b, 0, 0, 0)),
                  pl.BlockSpec((None, CONV_W - 1, CONV_DIM), lambda b, i: (b, 0, 0))],
        out_specs=[tok(D_INNER),
                   pl.BlockSpec((None, G_S, N_S, GROUP_W), lambda b, i: (b, 0, 0, 0)),
                   pl.BlockSpec((None, CONV_W - 1, CONV_DIM), lambda b, i: (b, 0, 0))],
        out_shape=[sds((B, L, D_INNER), BF16), sds((B, G_S, N_S, GROUP_W), F32), sds((B, CONV_W - 1, CONV_DIM), F32)],
        scratch_shapes=[pltpu.VMEM((ts + 8, CONV_DIM), F32), pltpu.VMEM((ts, CONV_DIM), F32),
                        pltpu.VMEM((ts, D_INNER), F32), pltpu.VMEM((ts, LANES), F32),
                        pltpu.VMEM((G_S, GROUP_W, GROUP_W), F32)],
        compiler_params=_params(("parallel", "arbitrary")),
        interpret=interpret,
        name="ssd",
    )(x, w_z, w_xbc, w_dt, conv_w.astype(F32), conv_b.astype(F32).reshape(1, CONV_DIM), pad(dt_bias), pad(a_log),
      dskip_e, ssd_norm.astype(F32).reshape(1, D_INNER), expand, h0_t, conv_hist.astype(F32))
    ssm_new = h_t.reshape(B, G_S, N_S, R_S, P_S).transpose(0, 1, 3, 4, 2).reshape(B, H_S, P_S, N_S)
    return y, ssm_new, conv_new


def _merge_kernel(x_ref, a_ref, s_ref, phist_ref, wpool_ref, poolw_ref, pscale_ref, wg_ref, wba_ref, wbs_ref,
                  wbp_ref, wo_ref, g_ref, b_ref, x1_ref, pnew_ref, ext_s, *, tm, pos0):
    i = pl.program_id(1)
    hrows = POOL_MAX - 1

    @pl.when(i == 0)
    def _():
        ext_s[POOL_MAX - hrows:POOL_MAX, :] = phist_ref[...]

    x = x_ref[...]
    xb = x.astype(BF16)
    u = _dot(xb, wpool_ref[...])
    ext_s[POOL_MAX:POOL_MAX + tm, :] = u
    pos = pos0 + i * tm + lax.broadcasted_iota(jnp.int32, (tm, 1), 0)
    outs = []
    for gi, w in enumerate(POOL_SIZES):
        cols = slice(gi * POOL_GW, (gi + 1) * POOL_GW)
        ug = u[:, cols]
        win = ug
        for kk in range(1, w):
            win = win + ext_s[POOL_MAX - kk:POOL_MAX - kk + tm, cols]
        cnt = jnp.minimum(pos + 1, w).astype(F32)
        d = win / cnt - ug
        outs.append(_dot(d.astype(BF16), poolw_ref[gi]) * pscale_ref[:, cols])
    pool_out = jnp.concatenate(outs, axis=1).astype(BF16)
    pnew = ext_s[tm + POOL_MAX - hrows:tm + POOL_MAX, :]
    pnew_ref[...] = pnew
    ext_s[POOL_MAX - hrows:POOL_MAX, :] = pnew

    merged = jax.nn.sigmoid(_dot(xb, wg_ref[:, 0:D_MODEL])) * _dot(a_ref[...], wba_ref[...])
    merged = merged + jax.nn.sigmoid(_dot(xb, wg_ref[:, D_MODEL:2 * D_MODEL])) * _dot(s_ref[...], wbs_ref[...])
    merged = merged + jax.nn.sigmoid(_dot(xb, wg_ref[:, 2 * D_MODEL:3 * D_MODEL])) * _dot(pool_out, wbp_ref[...])
    mix = _dot(merged.astype(BF16), wo_ref[...])
    x1_ref[...] = _layer_norm(ALPHA * x + mix, g_ref[...], b_ref[...])


def _merge_call(x, attn_out, ssd_out, pool_hist, w_pool, pool_w, pool_scale, w_g, w_ba, w_bs, w_bp, w_o, ln_g, ln_b,
                pos0, tm, interpret):
    B, L, D = x.shape
    kern = functools.partial(_merge_kernel, tm=tm, pos0=pos0)
    tok = lambda w: pl.BlockSpec((None, tm, w), lambda b, i: (b, i, 0))
    hist = pl.BlockSpec((None, POOL_MAX - 1, POOL_DIM), lambda b, i: (b, 0, 0))
    sds = jax.ShapeDtypeStruct
    return pl.pallas_call(
        kern,
        grid=(B, L // tm),
        in_specs=[tok(D), tok(ATTN_W), tok(D_INNER), hist,
                  _const_spec((D, POOL_DIM)), _const_spec((N_POOL, POOL_GW, POOL_GW)), _const_spec((1, POOL_DIM)),
                  _const_spec((D, N_BRANCH * D)), _const_spec((ATTN_W, D)), _const_spec((D_INNER, D)),
                  _const_spec((POOL_DIM, D)), _const_spec((D, D)), _const_spec((1, D)), _const_spec((1, D))],
        out_specs=[tok(D), hist],
        out_shape=[sds((B, L, D), F32), sds((B, POOL_MAX - 1, POOL_DIM), F32)],
        scratch_shapes=[pltpu.VMEM((tm + POOL_MAX, POOL_DIM), F32)],
        compiler_params=_params(("parallel", "arbitrary")),
        interpret=interpret,
        name="merge",
    )(x, attn_out, ssd_out, pool_hist.astype(F32), w_pool, pool_w, pool_scale.astype(F32).reshape(1, POOL_DIM),
      w_g, w_ba, w_bs, w_bp, w_o, ln_g.astype(F32).reshape(1, D), ln_b.astype(F32).reshape(1, D))


def _route(logits):
    lane = lax.broadcasted_iota(jnp.int32, logits.shape, 1)
    big = jnp.int32(LANES)
    gl = jnp.where(lane < E_GROUPS, logits, NEG_BIG)
    gmax = jnp.max(gl, axis=-1, keepdims=True)
    g_sel = jnp.min(jnp.where(gl == gmax, lane, big), axis=-1, keepdims=True)
    g_w = 1.0 / jnp.sum(jnp.exp(gl - gmax), axis=-1, keepdims=True)
    lo = E_GROUPS + g_sel * E_PER_GROUP
    in_grp = (lane >= lo) & (lane < lo + E_PER_GROUP)
    el = jnp.where(in_grp, logits, NEG_BIG)
    v1 = jnp.max(el, axis=-1, keepdims=True)
    i1 = jnp.min(jnp.where(el == v1, lane, big), axis=-1, keepdims=True)
    el2 = jnp.where(lane == i1, NEG_BIG, el)
    v2 = jnp.max(el2, axis=-1, keepdims=True)
    i2 = jnp.min(jnp.where(el2 == v2, lane, big), axis=-1, keepdims=True)
    e21 = jnp.exp(v2 - v1)
    w1 = g_w / (1.0 + e21)
    w2 = g_w * e21 / (1.0 + e21)
    return jnp.where(lane == i1, w1, 0.0) + jnp.where(lane == i2, w2, 0.0)


def _moe_kernel(x_ref, wr_ref, br_ref, wg_ref, wu_ref, wd_ref, g_ref, b_ref, o_ref, xb_s, comb_s, acc_s):
    e = pl.program_id(1)

    @pl.when(e == 0)
    def _():
        x = x_ref[...]
        xb_s[...] = x.astype(BF16)
        logits = jnp.dot(x, wr_ref[...], precision=HIGHEST, preferred_element_type=F32) + br_ref[...]
        comb_s[...] = _route(logits)
        acc_s[...] = jnp.zeros_like(acc_s)

    xb = xb_s[...]
    lane = lax.broadcasted_iota(jnp.int32, comb_s.shape, 1)
    w_e = jnp.sum(jnp.where(lane == e + E_GROUPS, comb_s[...], 0.0), axis=-1, keepdims=True)
    h = _silu(_dot(xb, wg_ref[...])) * _dot(xb, wu_ref[...]) * w_e
    acc_s[...] += _dot(h.astype(BF16), wd_ref[...])

    @pl.when(e == N_EXPERTS - 1)
    def _():
        o_ref[...] = _layer_norm(ALPHA * x_ref[...] + acc_s[...], g_ref[...], b_ref[...])


def _moe_call(x1, w_router, b_router, w_gate, w_up, w_down, ln_g, ln_b, tm, interpret):
    T, D = x1.shape
    tok = pl.BlockSpec((tm, D), lambda i, e: (i, 0))
    return pl.pallas_call(
        _moe_kernel,
        grid=(T // tm, N_EXPERTS),
        in_specs=[tok, _const_spec((D, LANES)), _const_spec((1, LANES)),
                  pl.BlockSpec((None, D, D_EXPERT), lambda i, e: (e, 0, 0)),
                  pl.BlockSpec((None, D, D_EXPERT), lambda i, e: (e, 0, 0)),
                  pl.BlockSpec((None, D_EXPERT, D), lambda i, e: (e, 0, 0)),
                  _const_spec((1, D)), _const_spec((1, D))],
        out_specs=tok,
        out_shape=jax.ShapeDtypeStruct((T, D), F32),
        scratch_shapes=[pltpu.VMEM((tm, D), BF16), pltpu.VMEM((tm, LANES), F32), pltpu.VMEM((tm, D), F32)],
        compiler_params=_params(("parallel", "arbitrary")),
        interpret=interpret,
        name="moe",
    )(x1, w_router, b_router, w_gate, w_up, w_down, ln_g.astype(F32).reshape(1, D), ln_b.astype(F32).reshape(1, D))


def _prep_layer_params(p):
    w_in = p['w_in']
    pad_dt = jnp.pad(w_in[:, OFF_DT:OFF_POOL], ((0, 0), (0, LANES - H_S)))
    w_router = jnp.pad(jnp.concatenate([p['rg_w'], p['re_w']], axis=1).astype(F32),
                       ((0, 0), (0, LANES - E_GROUPS - N_EXPERTS)))
    b_router = jnp.pad(jnp.concatenate([p['rg_b'], p['re_b']]).astype(F32),
                       (0, LANES - E_GROUPS - N_EXPERTS)).reshape(1, LANES)
    q = dict(p)
    q.update(
        w_qkv=w_in[:, 0:OFF_Z].astype(BF16), w_z=w_in[:, OFF_Z:OFF_XBC].astype(BF16),
        w_xbc=w_in[:, OFF_XBC:OFF_DT].astype(BF16), w_dt=pad_dt.astype(BF16),
        w_pool=w_in[:, OFF_POOL:OFF_GATE].astype(BF16), w_g=w_in[:, OFF_GATE:N_IN].astype(BF16),
        pool_w_b=p['pool_w'].astype(BF16), w_ba=p['w_br_attn'].astype(BF16), w_bs=p['w_br_ssd'].astype(BF16),
        w_bp=p['w_br_pool'].astype(BF16), w_o_b=p['w_o'].astype(BF16),
        w_router=w_router, b_router=b_router,
        w_gate_b=p['w_gate'].astype(BF16), w_up_b=p['w_up'].astype(BF16), w_down_b=p['w_down'].astype(BF16),
    )
    return q


def _pick(n, cap):
    t = min(n, cap)
    assert n % t == 0
    return t


def _layer(x, pos0, past_k, past_v, ssm_h0, conv_hist, pool_hist, p, layer_idx, interpret=False):
    B, L, D = x.shape
    assert L >= POOL_MAX and L % CHUNK == 0
    lam_init = 0.8 - 0.6 * math.exp(-0.3 * layer_idx)
    tm = _pick(L, 512)

    k, v, qb, kb, vb = _qkv_call(x, p['w_qkv'], pos0, tm, interpret)
    if past_k is None:
        attn_out = _attn_call(qb, kb, vb, p['lam'].astype(F32), p['attn_norm'].astype(F32), lam_init, True, interpret)
    else:
        n_past = past_k.shape[1]
        kb_all = jnp.concatenate([past_k.reshape(B, n_past, Q_COLS).astype(BF16), kb], axis=1)
        vb_all = jnp.concatenate([past_v.reshape(B, n_past, ATTN_W).astype(BF16), vb], axis=1)
        attn_out = _attn_call(qb, kb_all, vb_all, p['lam'].astype(F32), p['attn_norm'].astype(F32), lam_init, False,
                              interpret)

    ssd_out, ssm_new, conv_new = _ssd_call(x, p['w_z'], p['w_xbc'], p['w_dt'], p['conv_w'], p['conv_b'], p['dt_bias'],
                                           p['a_log'], p['d_skip'], p['ssd_norm'], ssm_h0, conv_hist,
                                           tm, _pick(L, 128), interpret)

    x1, pool_new = _merge_call(x, attn_out, ssd_out, pool_hist, p['w_pool'], p['pool_w_b'], p['pool_scale'], p['w_g'],
                               p['w_ba'], p['w_bs'], p['w_bp'], p['w_o_b'], p['ln1_g'], p['ln1_b'], pos0, tm, interpret)

    T = B * L
    x2 = _moe_call(x1.reshape(T, D), p['w_router'], p['b_router'], p['w_gate_b'], p['w_up_b'], p['w_down_b'],
                   p['ln2_g'], p['ln2_b'], _pick(T, 1024), interpret)
    return x2.reshape(B, L, D), (k.reshape(B, L, H_A, 2 * DH_A), v.reshape(B, L, H_A, DV_A), ssm_new, conv_new, pool_new)


def kernel(x_prompt, x_sample, cache_k, cache_v, state_ssm, state_conv, state_pool, w_in, conv_w, conv_b, dt_bias,
           a_log, d_skip, ssd_norm, lam, attn_norm, pool_w, pool_scale, w_br_attn, w_br_ssd, w_br_pool, w_o, ln1_g,
           ln1_b, router_group_w, router_group_b, router_expert_w, router_expert_b, w_gate, w_up, w_down, ln2_g, ln2_b):
    bp = x_prompt.shape[0]
    depth = w_in.shape[0]
    zero_h = jnp.zeros((bp, H_S, P_S, N_S), F32)
    zero_conv = jnp.zeros((bp, CONV_W - 1, CONV_DIM), F32)
    zero_pool = jnp.zeros((bp, POOL_MAX - 1, POOL_DIM), F32)
    pos0_sample = cache_k.shape[2]
    hp, hs = x_prompt, x_sample
    st_p, st_s = [], []
    for l in range(depth):
        p = _prep_layer_params({
            'w_in': w_in[l], 'conv_w': conv_w[l], 'conv_b': conv_b[l], 'dt_bias': dt_bias[l], 'a_log': a_log[l],
            'd_skip': d_skip[l], 'ssd_norm': ssd_norm[l], 'lam': lam[l], 'attn_norm': attn_norm[l],
            'pool_w': pool_w[l], 'pool_scale': pool_scale[l], 'w_br_attn': w_br_attn[l], 'w_br_ssd': w_br_ssd[l],
            'w_br_pool': w_br_pool[l], 'w_o': w_o[l], 'ln1_g': ln1_g[l], 'ln1_b': ln1_b[l],
            'rg_w': router_group_w[l], 'rg_b': router_group_b[l], 're_w': router_expert_w[l],
            're_b': router_expert_b[l], 'w_gate': w_gate[l], 'w_up': w_up[l], 'w_down': w_down[l],
            'ln2_g': ln2_g[l], 'ln2_b': ln2_b[l],
        })
        hp, sp = _layer(hp, 0, None, None, zero_h, zero_conv, zero_pool, p, l)
        hs, ss = _layer(hs, pos0_sample, cache_k[l], cache_v[l], state_ssm[l], state_conv[l], state_pool[l], p, l)
        st_p.append(sp)
        st_s.append(ss)
    stack = lambda sts, j: jnp.stack([s[j] for s in sts])
    return (hp, hs) + tuple(stack(st_p, j) for j in range(5)) + tuple(stack(st_s, j) for j in range(5))
```

```python
import functools
import math

import jax
import jax.numpy as jnp
from jax import lax
from jax.experimental import pallas as pl
from jax.experimental.pallas import tpu as pltpu

F32 = jnp.float32
BF16 = jnp.bfloat16
HIGHEST = lax.Precision.HIGHEST

D_MODEL = 1024
CHUNK = 64
H_A = 4
DH_A = 64
DV_A = 2 * DH_A
ROT_DIM = DH_A // 4
ROPE_THETA = 500000.0
ATTN_W = H_A * DV_A
D_INNER = D_MODEL
P_S = 64
H_S = D_INNER // P_S
G_S = 4
R_S = H_S // G_S
N_S = 64
CONV_W = 4
CONV_DIM = D_INNER + 2 * G_S * N_S
POOL_SIZES = (2, 4, 8, 16)
N_POOL = 4
POOL_GW = D_MODEL // 8
POOL_DIM = N_POOL * POOL_GW
POOL_MAX = 16
N_BRANCH = 3
Q_COLS = H_A * 2 * DH_A
OFF_K = Q_COLS
OFF_V = OFF_K + Q_COLS
OFF_Z = OFF_V + H_A * DV_A
OFF_XBC = OFF_Z + D_INNER
OFF_DT = OFF_XBC + CONV_DIM
OFF_POOL = OFF_DT + H_S
OFF_GATE = OFF_POOL + POOL_DIM
N_IN = OFF_GATE + N_BRANCH * D_MODEL
E_GROUPS = 4
E_PER_GROUP = 4
N_EXPERTS = E_GROUPS * E_PER_GROUP
D_EXPERT = D_MODEL // 2
DEPTH_ = 2
ALPHA = (2 * DEPTH_) ** 0.25
LN_EPS = 1e-5
RMS_EPS = 1e-5

LANES = 128
GROUP_W = R_S * P_S
VMEM_LIMIT = 56 * 1024 * 1024
NEG_BIG = -1e30


def _const_spec(shape):
    nd = len(shape)
    return pl.BlockSpec(shape, lambda *_: (0,) * nd)


def _params(sem):
    return pltpu.CompilerParams(dimension_semantics=sem, vmem_limit_bytes=VMEM_LIMIT)


def _dot(a, b):
    return jnp.dot(a, b, preferred_element_type=F32)


def _dot_nt(a, b):
    return lax.dot_general(a, b, (((1,), (1,)), ((), ())), preferred_element_type=F32)


def _dot_tn(a, b):
    return lax.dot_general(a, b, (((0,), (0,)), ((), ())), preferred_element_type=F32)


def _split_bf16(v):
    hi = v.astype(BF16)
    return hi, (v - hi.astype(F32)).astype(BF16)


def _silu(x):
    return x * jax.nn.sigmoid(x)


def _layer_norm(x, g, b):
    mu = jnp.mean(x, axis=-1, keepdims=True)
    xc = x - mu
    var = jnp.mean(xc * xc, axis=-1, keepdims=True)
    return xc * lax.rsqrt(var + LN_EPS) * g + b


def _qkv_kernel(x_ref, w_ref, cos_ref, sa_ref, sb_ref, k_ref, v_ref, qb_ref, kb_ref, vb_ref, *, tb):
    xb = x_ref[...].astype(BF16)
    qkv = _dot(xb, w_ref[...])
    c, sa, sb = cos_ref[...], sa_ref[...], sb_ref[...]

    def rope(t):
        heads = []
        for h in range(H_A):
            th = t[:, h * LANES:(h + 1) * LANES]
            heads.append(th * c + pltpu.roll(th, LANES - ROT_DIM // 2, 1) * sa + pltpu.roll(th, ROT_DIM // 2, 1) * sb)
        return jnp.concatenate(heads, axis=1)

    q = rope(qkv[:, 0:OFF_K]) * (DH_A ** -0.5)
    k = rope(qkv[:, OFF_K:OFF_V])
    v = qkv[:, OFF_V:OFF_Z]
    k_ref[...] = k
    v_ref[...] = v
    kb_ref[...] = k.astype(BF16)
    if tb is None:
        qb_ref[...] = q.astype(BF16)
        vb_ref[...] = v.astype(BF16)
    else:
        for s in range(q.shape[0] // tb):
            qb_ref[s] = q[s * tb:(s + 1) * tb, :].T.astype(BF16)
            vb_ref[s] = v[s * tb:(s + 1) * tb, :].T.astype(BF16)


def _rope_tables(pos0, L):
    half = ROT_DIM // 2
    inv = 1.0 / (ROPE_THETA ** (jnp.arange(half, dtype=F32) / half))
    pos = pos0 + jnp.arange(L, dtype=jnp.int32)
    ang = pos.astype(F32)[:, None] * inv[None, :]
    cos, sin = jnp.cos(ang), jnp.sin(ang)
    rest = DH_A - ROT_DIM
    c64 = jnp.concatenate([cos, cos, jnp.ones((L, rest), F32)], axis=1)
    sa64 = jnp.concatenate([-sin, jnp.zeros((L, DH_A - half), F32)], axis=1)
    sb64 = jnp.concatenate([jnp.zeros((L, half), F32), sin, jnp.zeros((L, rest), F32)], axis=1)
    two = lambda t: jnp.concatenate([t, t], axis=1)
    return two(c64), two(sa64), two(sb64)


def _qkv_call(x, w_qkv, pos0, tm, tb, interpret):
    B, L, D = x.shape
    cos_t, sa_t, sb_t = _rope_tables(pos0, L)
    tok = lambda w: pl.BlockSpec((None, tm, w), lambda b, i: (b, i, 0))
    tab = pl.BlockSpec((tm, LANES), lambda b, i: (i, 0))
    sds = jax.ShapeDtypeStruct
    if tb is None:
        t_spec, t_shape = tok(Q_COLS), sds((B, L, Q_COLS), BF16)
    else:
        t_spec = pl.BlockSpec((None, tm // tb, Q_COLS, tb), lambda b, i: (b, i, 0, 0))
        t_shape = sds((B, L // tb, Q_COLS, tb), BF16)
    return pl.pallas_call(
        functools.partial(_qkv_kernel, tb=tb),
        grid=(B, L // tm),
        in_specs=[tok(D), _const_spec((D, OFF_Z)), tab, tab, tab],
        out_specs=[tok(Q_COLS), tok(Q_COLS), t_spec, tok(Q_COLS), t_spec],
        out_shape=[sds((B, L, Q_COLS), F32)] * 2 + [t_shape, sds((B, L, Q_COLS), BF16), t_shape],
        compiler_params=_params(("parallel", "parallel")),
        interpret=interpret,
        name="qkv_rope",
    )(x, w_qkv, cos_t, sa_t, sb_t)


def _attn_kernel(qt_ref, k_ref, vt_ref, lam_ref, nrm_ref, o_ref, *scratch, tb, n_full, causal, n_valid, lam_init):
    qi = pl.program_id(1)
    s_s, p_s, acc_s = scratch[0:H_A], scratch[H_A:2 * H_A], scratch[2 * H_A:3 * H_A]
    sub = lax.broadcasted_iota(jnp.int32, (LANES, tb), 0)
    key = lax.broadcasted_iota(jnp.int32, (tb, 2 * tb), 0)
    if causal:
        qry = lax.broadcasted_iota(jnp.int32, (tb, 2 * tb), 1) & (tb - 1)
        shift = CHUNK.bit_length() - 1
        last_mask = lax.shift_right_logical(key, shift) <= lax.shift_right_logical(qry, shift)
        n_loop = qi
    else:
        last_mask = (key < n_valid) if n_valid < tb else None
        n_loop = n_full

    lv = lam_ref[...]
    lam = (jnp.exp(jnp.sum(lv[0:1] * lv[1:2], axis=-1, keepdims=True))
           - jnp.exp(jnp.sum(lv[2:3] * lv[3:4], axis=-1, keepdims=True)) + lam_init)

    heads = [slice(h * LANES, (h + 1) * LANES) for h in range(H_A)]
    q2t = []
    for hs in heads:
        qt = qt_ref[hs, :]
        zero = jnp.zeros_like(qt)
        q2t.append(jnp.concatenate([jnp.where(sub < DH_A, qt, zero), jnp.where(sub >= DH_A, qt, zero)], axis=1))

    def scores(h, j):
        r0 = j * tb if isinstance(j, int) else pl.multiple_of(j * tb, tb)
        return _dot(k_ref[pl.ds(r0, tb), heads[h]], q2t[h])

    def softmax_step(s, m, l):
        m_new = jnp.maximum(m, jnp.max(s, axis=0, keepdims=True))
        alpha = jnp.exp(m - m_new)
        p = jnp.exp(s - m_new)
        return m_new, alpha * l + jnp.sum(p, axis=0, keepdims=True), alpha, p.astype(BF16)

    def body(j, carry):
        slot = j & 1
        out = []
        for h in range(H_A):
            m, l = carry[h]
            pv_prev = _dot(vt_ref[jnp.maximum(j - 1, 0), heads[h], :], p_s[h][1 - slot])
            m, l, alpha, p = softmax_step(s_s[h][slot], m, l)
            p_s[h][slot] = p
            acc_s[h][...] = alpha * (acc_s[h][...] + pv_prev)
            s_s[h][1 - slot] = scores(h, j + 1)
            out.append((m, l))
        return tuple(out)

    for h in range(H_A):
        s_s[h][0] = scores(h, 0)
        p_s[h][1] = jnp.zeros((tb, 2 * tb), BF16)
        acc_s[h][...] = jnp.zeros((LANES, 2 * tb), F32)
    init = (jnp.full((1, 2 * tb), NEG_BIG, F32), jnp.zeros((1, 2 * tb), F32))
    carry = lax.fori_loop(0, n_loop, body, (init,) * H_A)
    last = n_loop & 1
    for h in range(H_A):
        m, l = carry[h]
        pv_prev = _dot(vt_ref[jnp.maximum(n_loop - 1, 0), heads[h], :], p_s[h][1 - last])
        s_last = s_s[h][last]
        if last_mask is not None:
            s_last = jnp.where(last_mask, s_last, NEG_BIG)
        m, l, alpha, p = softmax_step(s_last, m, l)
        acc = alpha * (acc_s[h][...] + pv_prev) + _dot(vt_ref[n_loop, heads[h], :], p)
        on = acc * (1.0 / l)
        ot = on[:, 0:tb] - lam * on[:, tb:2 * tb]
        ot = ot * lax.rsqrt(jnp.mean(ot * ot, axis=0, keepdims=True) + RMS_EPS) * nrm_ref[...] * (1.0 - lam_init)
        o_ref[:, heads[h]] = ot.T.astype(BF16)


def _attn_call(qt, kb, vt, lam, attn_norm, lam_init, causal, n_valid, interpret):
    B, nq, _, tb = qt.shape
    nk = vt.shape[1]
    Lk = kb.shape[1]
    assert Lk == nk * tb and tb % CHUNK == 0 and (not causal or nq == nk)
    kern = functools.partial(_attn_kernel, tb=tb, n_full=nk - 1, causal=causal, n_valid=n_valid, lam_init=lam_init)
    nrm = jnp.broadcast_to(attn_norm.astype(F32).reshape(DV_A, 1), (DV_A, tb))
    return pl.pallas_call(
        kern,
        grid=(B, nq),
        in_specs=[pl.BlockSpec((None, None, ATTN_W, tb), lambda b, i: (b, i, 0, 0)),
                  pl.BlockSpec((None, Lk, Q_COLS), lambda b, i: (b, 0, 0)),
                  pl.BlockSpec((None, nk, ATTN_W, tb), lambda b, i: (b, 0, 0, 0)),
                  _const_spec((4, DH_A)), _const_spec((DV_A, tb))],
        out_specs=pl.BlockSpec((None, tb, ATTN_W), lambda b, i: (b, i, 0)),
        out_shape=jax.ShapeDtypeStruct((B, nq * tb, ATTN_W), BF16),
        scratch_shapes=([pltpu.VMEM((2, tb, 2 * tb), F32)] * H_A + [pltpu.VMEM((2, tb, 2 * tb), BF16)] * H_A
                        + [pltpu.VMEM((LANES, 2 * tb), F32)] * H_A),
        compiler_params=_params(("parallel", "arbitrary")),
        interpret=interpret,
        name="diff_attn",
    )(qt, kb, vt, lam, nrm)


def _ssd_kernel(x_ref, wz_ref, wxbc_ref, wdt_ref, convw_ref, convb_ref, dtb_ref, alog_ref, dskip_ref, nrm_ref,
                expand_ref, h0_ref, chist_ref,
                y_ref, hout_ref, cnew_ref,
                ext_s, xbc_s, z_s, dt_s, h_s, *, ts, q):
    i = pl.program_id(1)
    hist0 = 8 - (CONV_W - 1)

    @pl.when(i == 0)
    def _():
        h_s[...] = jnp.zeros_like(h_s)
        for g in range(G_S):
            h_s[g, g * N_S:(g + 1) * N_S, :] = h0_ref[g]
        ext_s[hist0:8, :] = chist_ref[...]

    xb = x_ref[...].astype(BF16)
    z_s[...] = _dot(xb, wz_ref[...])
    ext_s[8:8 + ts, :] = _dot(xb, wxbc_ref[...])
    dt_raw = _dot(xb, wdt_ref[...])
    x_dt = dt_raw + dtb_ref[...]
    dt_s[...] = jnp.maximum(x_dt, 0.0) + jnp.log1p(jnp.exp(-jnp.abs(x_dt)))

    acc = convb_ref[...] + ext_s[hist0:hist0 + ts, :] * convw_ref[0:1, :]
    for tap in range(1, CONV_W):
        acc = acc + ext_s[hist0 + tap:hist0 + tap + ts, :] * convw_ref[tap:tap + 1, :]
    xbc_s[...] = _silu(acc)
    hist_new = ext_s[ts + hist0:ts + 8, :]
    cnew_ref[...] = hist_new
    ext_s[hist0:8, :] = hist_new

    a_row = -jnp.exp(alog_ref[...])
    ri = lax.broadcasted_iota(jnp.int32, (q, q), 0)
    ci = lax.broadcasted_iota(jnp.int32, (q, q), 1)
    causal = ri >= ci
    tril = jnp.where(causal, 1.0, 0.0).astype(F32)
    lane_g = lax.shift_right_logical(lax.broadcasted_iota(jnp.int32, (q, GROUP_W), 1),
                                     N_S.bit_length() - 1)

    def chunk(c, _):
        r0 = pl.multiple_of(c * q, q)
        rows = pl.ds(r0, q)
        dtc = dt_s[rows, :]
        acs = jnp.dot(tril, dtc * a_row, precision=HIGHEST, preferred_element_type=F32)
        acs_last = acs[q - 1:q, :]
        stack = jnp.concatenate([dtc * jnp.exp(acs_last - acs), jnp.exp(acs),
                                 jnp.broadcast_to(jnp.exp(acs_last), (8, LANES))], axis=0)
        s_hi, s_lo = _split_bf16(stack)
        ex = _dot(s_hi, expand_ref[...]) + _dot(s_lo, expand_ref[...])
        w_e, ein_e, cd_e = ex[0:q], ex[q:2 * q], ex[2 * q:2 * q + 1]
        xs = xbc_s[rows, 0:D_INNER]
        bm = xbc_s[rows, D_INNER:D_INNER + GROUP_W]
        cm = xbc_s[rows, D_INNER + GROUP_W:CONV_DIM]
        xs_b = xs.astype(BF16)
        xdec_b = (xs * w_e).astype(BF16)
        bm_b = bm.astype(BF16)
        acs_t = acs.T
        dt_t = dtc.T
        zero_bc = jnp.zeros((q, GROUP_W), F32)
        zero_x = jnp.zeros((q, GROUP_W), BF16)
        ys = []
        for g in range(G_S):
            gs = slice(g * GROUP_W, (g + 1) * GROUP_W)
            cm_g = jnp.where(lane_g == g, cm, zero_bc).astype(BF16)
            bm_g = jnp.where(lane_g == g, bm, zero_bc).astype(BF16)
            cb = _dot_nt(cm_g, bm_b)
            xs_g = xs_b[:, gs]
            y_g = _dot(cm_g, h_s[g].astype(BF16)) * ein_e[:, gs]
            for r in range(R_S):
                h = g * R_S + r
                seg = acs[:, h:h + 1] - acs_t[h:h + 1, :]
                lm = jnp.exp(jnp.where(causal, seg, NEG_BIG))
                m_h = (cb * lm * dt_t[h:h + 1, :]).astype(BF16)
                y_g = y_g + _dot(m_h, jnp.where(lane_g == r, xs_g, zero_x))
            ys.append(y_g)
            h_s[g] = h_s[g] * cd_e[:, gs] + _dot_tn(bm_g, xdec_b[:, gs])
        y = jnp.concatenate(ys, axis=1) + dskip_ref[...] * xs
        y = y * _silu(z_s[rows, :])
        y = y * lax.rsqrt(jnp.mean(y * y, axis=-1, keepdims=True) + RMS_EPS) * nrm_ref[...]
        y_ref[rows, :] = y.astype(BF16)
        return 0

    lax.fori_loop(0, ts // q, chunk, 0)
    for g in range(G_S):
        hout_ref[g] = h_s[g, g * N_S:(g + 1) * N_S, :]


def _ssd_call(x, w_z, w_xbc, w_dt, conv_w, conv_b, dt_bias, a_log, d_skip, ssd_norm, h0, conv_hist, ts, q, interpret):
    B, L, D = x.shape
    pad = lambda v: jnp.pad(v.astype(F32), (0, LANES - H_S)).reshape(1, LANES)
    expand = (jnp.arange(LANES)[:, None] == (jnp.arange(D_INNER)[None, :] // P_S)).astype(BF16)
    dskip_e = jnp.repeat(d_skip.astype(F32), P_S).reshape(1, D_INNER)
    h0_t = h0.astype(F32).reshape(B, G_S, R_S, P_S, N_S).transpose(0, 1, 4, 2, 3).reshape(B, G_S, N_S, GROUP_W)
    kern = functools.partial(_ssd_kernel, ts=ts, q=q)
    tok = lambda w: pl.BlockSpec((None, ts, w), lambda b, i: (b, i, 0))
    sds = jax.ShapeDtypeStruct
    y, h_t, conv_new = pl.pallas_call(
        kern,
        grid=(B, L // ts),
        in_specs=[tok(D), _const_spec((D, D_INNER)), _const_spec((D, CONV_DIM)), _const_spec((D, LANES)),
                  _const_spec((CONV_W, CONV_DIM)), _const_spec((1, CONV_DIM)), _const_spec((1, LANES)),
                  _const_spec((1, LANES)), _const_spec((1, D_INNER)), _const_spec((1, D_INNER)),
                  _const_spec((LANES, D_INNER)),
                  pl.BlockSpec((None, G_S, N_S, GROUP_W), lambda b, i: (b, 0, 0, 0)),
                  pl.BlockSpec((None, CONV_W - 1, CONV_DIM), lambda b, i: (b, 0, 0))],
        out_specs=[tok(D_INNER),
                   pl.BlockSpec((None, G_S, N_S, GROUP_W), lambda b, i: (b, 0, 0, 0)),
                   pl.BlockSpec((None, CONV_W - 1, CONV_DIM), lambda b, i: (b, 0, 0))],
        out_shape=[sds((B, L, D_INNER), BF16), sds((B, G_S, N_S, GROUP_W), F32), sds((B, CONV_W - 1, CONV_DIM), F32)],
        scratch_shapes=[pltpu.VMEM((ts + 8, CONV_DIM), F32), pltpu.VMEM((ts, CONV_DIM), F32),
                        pltpu.VMEM((ts, D_INNER), F32), pltpu.VMEM((ts, LANES), F32),
                        pltpu.VMEM((G_S, GROUP_W, GROUP_W), F32)],
        compiler_params=_params(("parallel", "arbitrary")),
        interpret=interpret,
        name="ssd",
    )(x, w_z, w_xbc, w_dt, conv_w.astype(F32), conv_b.astype(F32).reshape(1, CONV_DIM), pad(dt_bias), pad(a_log),
      dskip_e, ssd_norm.astype(F32).reshape(1, D_INNER), expand, h0_t, conv_hist.astype(F32))
    ssm_new = h_t.reshape(B, G_S, N_S, R_S, P_S).transpose(0, 1, 3, 4, 2).reshape(B, H_S, P_S, N_S)
    return y, ssm_new, conv_new


def _merge_kernel(x_ref, a_ref, s_ref, phist_ref, wpool_ref, poolw_ref, pscale_ref, wg_ref, wba_ref, wbs_ref,
                  wbp_ref, wo_ref, g_ref, b_ref, x1_ref, pnew_ref, ext_s, *, tm, pos0):
    i = pl.program_id(1)
    hrows = POOL_MAX - 1

    @pl.when(i == 0)
    def _():
        ext_s[POOL_MAX - hrows:POOL_MAX, :] = phist_ref[...]

    x = x_ref[...]
    xb = x.astype(BF16)
    u = _dot(xb, wpool_ref[...])
    ext_s[POOL_MAX:POOL_MAX + tm, :] = u
    pos = pos0 + i * tm + lax.broadcasted_iota(jnp.int32, (tm, 1), 0)
    outs = []
    for gi, w in enumerate(POOL_SIZES):
        cols = slice(gi * POOL_GW, (gi + 1) * POOL_GW)
        ug = u[:, cols]
        win = ug
        for kk in range(1, w):
            win = win + ext_s[POOL_MAX - kk:POOL_MAX - kk + tm, cols]
        cnt = jnp.minimum(pos + 1, w).astype(F32)
        d = win / cnt - ug
        outs.append(_dot(d.astype(BF16), poolw_ref[gi]) * pscale_ref[:, cols])
    pool_out = jnp.concatenate(outs, axis=1).astype(BF16)
    pnew = ext_s[tm + POOL_MAX - hrows:tm + POOL_MAX, :]
    pnew_ref[...] = pnew
    ext_s[POOL_MAX - hrows:POOL_MAX, :] = pnew

    merged = jax.nn.sigmoid(_dot(xb, wg_ref[:, 0:D_MODEL])) * _dot(a_ref[...], wba_ref[...])
    merged = merged + jax.nn.sigmoid(_dot(xb, wg_ref[:, D_MODEL:2 * D_MODEL])) * _dot(s_ref[...], wbs_ref[...])
    merged = merged + jax.nn.sigmoid(_dot(xb, wg_ref[:, 2 * D_MODEL:3 * D_MODEL])) * _dot(pool_out, wbp_ref[...])
    mix = _dot(merged.astype(BF16), wo_ref[...])
    x1_ref[...] = _layer_norm(ALPHA * x + mix, g_ref[...], b_ref[...])


def _merge_call(x, attn_out, ssd_out, pool_hist, w_pool, pool_w, pool_scale, w_g, w_ba, w_bs, w_bp, w_o, ln_g, ln_b,
                pos0, tm, interpret):
    B, L, D = x.shape
    kern = functools.partial(_merge_kernel, tm=tm, pos0=pos0)
    tok = lambda w: pl.BlockSpec((None, tm, w), lambda b, i: (b, i, 0))
    hist = pl.BlockSpec((None, POOL_MAX - 1, POOL_DIM), lambda b, i: (b, 0, 0))
    sds = jax.ShapeDtypeStruct
    return pl.pallas_call(
        kern,
        grid=(B, L // tm),
        in_specs=[tok(D), tok(ATTN_W), tok(D_INNER), hist,
                  _const_spec((D, POOL_DIM)), _const_spec((N_POOL, POOL_GW, POOL_GW)), _const_spec((1, POOL_DIM)),
                  _const_spec((D, N_BRANCH * D)), _const_spec((ATTN_W, D)), _const_spec((D_INNER, D)),
                  _const_spec((POOL_DIM, D)), _const_spec((D, D)), _const_spec((1, D)), _const_spec((1, D))],
        out_specs=[tok(D), hist],
        out_shape=[sds((B, L, D), F32), sds((B, POOL_MAX - 1, POOL_DIM), F32)],
        scratch_shapes=[pltpu.VMEM((tm + POOL_MAX, POOL_DIM), F32)],
        compiler_params=_params(("parallel", "arbitrary")),
        interpret=interpret,
        name="merge",
    )(x, attn_out, ssd_out, pool_hist.astype(F32), w_pool, pool_w, pool_scale.astype(F32).reshape(1, POOL_DIM),
      w_g, w_ba, w_bs, w_bp, w_o, ln_g.astype(F32).reshape(1, D), ln_b.astype(F32).reshape(1, D))


def _route(logits):
    lane = lax.broadcasted_iota(jnp.int32, logits.shape, 1)
    big = jnp.int32(LANES)
    gl = jnp.where(lane < E_GROUPS, logits, NEG_BIG)
    gmax = jnp.max(gl, axis=-1, keepdims=True)
    g_sel = jnp.min(jnp.where(gl == gmax, lane, big), axis=-1, keepdims=True)
    g_w = 1.0 / jnp.sum(jnp.exp(gl - gmax), axis=-1, keepdims=True)
    lo = E_GROUPS + g_sel * E_PER_GROUP
    in_grp = (lane >= lo) & (lane < lo + E_PER_GROUP)
    el = jnp.where(in_grp, logits, NEG_BIG)
    v1 = jnp.max(el, axis=-1, keepdims=True)
    i1 = jnp.min(jnp.where(el == v1, lane, big), axis=-1, keepdims=True)
    el2 = jnp.where(lane == i1, NEG_BIG, el)
    v2 = jnp.max(el2, axis=-1, keepdims=True)
    i2 = jnp.min(jnp.where(el2 == v2, lane, big), axis=-1, keepdims=True)
    e21 = jnp.exp(v2 - v1)
    w1 = g_w / (1.0 + e21)
    w2 = g_w * e21 / (1.0 + e21)
    return jnp.where(lane == i1, w1, 0.0) + jnp.where(lane == i2, w2, 0.0), g_sel


def _moe_kernel(x_ref, wr_ref, br_ref, wg_ref, wu_ref, wd_ref, g_ref, b_ref, o_ref,
                xs_s, comb_s, p_s, pt_s, cum_s, seg_s, *, tm, sb, rc):
    g = pl.program_id(1)

    @pl.when(g == 0)
    def _():
        x = x_ref[...]
        x_hi, x_lo = _split_bf16(x)
        lg = _dot(x_hi, wr_ref[...])
        logits = lg[:, 0:LANES] + lg[:, LANES:2 * LANES] + _dot(x_lo, wr_ref[:, 0:LANES]) + br_ref[...]
        comb, g_sel = _route(logits)
        lane = lax.broadcasted_iota(jnp.int32, (tm, LANES), 1)
        onehot = jnp.where(lane == g_sel, 1.0, 0.0)
        onehot_b = onehot.astype(BF16)
        for c in range(tm // rc):
            ri = c * rc + lax.broadcasted_iota(jnp.int32, (rc, tm), 0)
            ci = lax.broadcasted_iota(jnp.int32, (rc, tm), 1)
            tri = jnp.where(ci < ri, 1.0, 0.0).astype(BF16)
            cum_s[c * rc:(c + 1) * rc, :] = _dot(tri, onehot_b)
        counts = jnp.sum(onehot, axis=0, keepdims=True)
        lane1 = lax.broadcasted_iota(jnp.int32, (1, LANES), 1)
        off = jnp.zeros((1, LANES), F32)
        start = jnp.float32(0.0)
        for gg in range(E_GROUPS):
            cnt = jnp.sum(jnp.where(lane1 == gg, counts, 0.0))
            seg_s[gg] = start.astype(jnp.int32)
            seg_s[E_GROUPS + gg] = (start + cnt).astype(jnp.int32)
            start = start + cnt
            off = off + jnp.where(lane1 > gg, cnt, 0.0)
        dest = jnp.sum(onehot * (off + cum_s[...]), axis=-1, keepdims=True)
        dest_i = dest.astype(jnp.int32)
        dest_row = jnp.broadcast_to(dest, (tm, LANES)).T[0:1, :].astype(jnp.int32)
        for c in range(tm // rc):
            rows = slice(c * rc, (c + 1) * rc)
            ri = c * rc + lax.broadcasted_iota(jnp.int32, (rc, tm), 0)
            ci = lax.broadcasted_iota(jnp.int32, (rc, tm), 1)
            p_s[rows, :] = jnp.where(ri == dest_row, 1.0, 0.0).astype(BF16)
            pt_s[rows, :] = jnp.where(ci == dest_i[rows, :], 1.0, 0.0).astype(BF16)
        p = p_s[...]
        xs_s[...] = _dot(p, x_hi).astype(BF16)
        c_hi, c_lo = _split_bf16(comb)
        comb_s[...] = _dot(p, c_hi) + _dot(p, c_lo)
        o_ref[...] = jnp.zeros_like(o_ref)

    start = seg_s[g]
    end = seg_s[E_GROUPS + g]
    for blk in range(tm // sb):
        @pl.when((start < (blk + 1) * sb) & (end > blk * sb))
        def _():
            rows = slice(blk * sb, (blk + 1) * sb)
            xb = xs_s[rows, :]
            cw = comb_s[rows, :]
            lane = lax.broadcasted_iota(jnp.int32, (sb, LANES), 1)
            upd = jnp.zeros((sb, D_MODEL), F32)
            for j in range(E_PER_GROUP):
                e_lane = E_GROUPS + g * E_PER_GROUP + j
                w_e = jnp.sum(jnp.where(lane == e_lane, cw, 0.0), axis=-1, keepdims=True)
                h = _silu(_dot(xb, wg_ref[j])) * _dot(xb, wu_ref[j]) * w_e
                upd = upd + _dot(h.astype(BF16), wd_ref[j])
            o_ref[rows, :] += upd

    @pl.when(g == E_GROUPS - 1)
    def _():
        a_hi, a_lo = _split_bf16(o_ref[...])
        pt = pt_s[...]
        ff = _dot(pt, a_hi) + _dot(pt, a_lo)
        o_ref[...] = _layer_norm(ALPHA * x_ref[...] + ff, g_ref[...], b_ref[...])


def _moe_call(x1, w_router, b_router, w_gate, w_up, w_down, ln_g, ln_b, tm, interpret):
    T, D = x1.shape
    tok = pl.BlockSpec((tm, D), lambda i, g: (i, 0))
    sb = _pick(tm, 128)
    kern = functools.partial(_moe_kernel, tm=tm, sb=sb, rc=_pick(tm, 128))
    return pl.pallas_call(
        kern,
        grid=(T // tm, E_GROUPS),
        in_specs=[tok, _const_spec((D, 2 * LANES)), _const_spec((1, LANES)),
                  pl.BlockSpec((E_PER_GROUP, D, D_EXPERT), lambda i, g: (g, 0, 0)),
                  pl.BlockSpec((E_PER_GROUP, D, D_EXPERT), lambda i, g: (g, 0, 0)),
                  pl.BlockSpec((E_PER_GROUP, D_EXPERT, D), lambda i, g: (g, 0, 0)),
                  _const_spec((1, D)), _const_spec((1, D))],
        out_specs=tok,
        out_shape=jax.ShapeDtypeStruct((T, D), F32),
        scratch_shapes=[pltpu.VMEM((tm, D), BF16), pltpu.VMEM((tm, LANES), F32),
                        pltpu.VMEM((tm, tm), BF16), pltpu.VMEM((tm, tm), BF16), pltpu.VMEM((tm, LANES), F32),
                        pltpu.SMEM((2 * E_GROUPS,), jnp.int32)],
        compiler_params=_params(("parallel", "arbitrary")),
        interpret=interpret,
        name="moe",
    )(x1, w_router, b_router, w_gate, w_up, w_down, ln_g.astype(F32).reshape(1, D), ln_b.astype(F32).reshape(1, D))


def _prep_layer_params(p):
    w_in = p['w_in']
    pad_dt = jnp.pad(w_in[:, OFF_DT:OFF_POOL], ((0, 0), (0, LANES - H_S)))
    w_router = jnp.pad(jnp.concatenate([p['rg_w'], p['re_w']], axis=1).astype(F32),
                       ((0, 0), (0, LANES - E_GROUPS - N_EXPERTS)))
    w_router = jnp.concatenate(_split_bf16(w_router), axis=1)
    b_router = jnp.pad(jnp.concatenate([p['rg_b'], p['re_b']]).astype(F32),
                       (0, LANES - E_GROUPS - N_EXPERTS)).reshape(1, LANES)
    q = dict(p)
    q.update(
        w_qkv=w_in[:, 0:OFF_Z].astype(BF16), w_z=w_in[:, OFF_Z:OFF_XBC].astype(BF16),
        w_xbc=w_in[:, OFF_XBC:OFF_DT].astype(BF16), w_dt=pad_dt.astype(BF16),
        w_pool=w_in[:, OFF_POOL:OFF_GATE].astype(BF16), w_g=w_in[:, OFF_GATE:N_IN].astype(BF16),
        pool_w_b=p['pool_w'].astype(BF16), w_ba=p['w_br_attn'].astype(BF16), w_bs=p['w_br_ssd'].astype(BF16),
        w_bp=p['w_br_pool'].astype(BF16), w_o_b=p['w_o'].astype(BF16),
        w_router=w_router, b_router=b_router,
        w_gate_b=p['w_gate'].astype(BF16), w_up_b=p['w_up'].astype(BF16), w_down_b=p['w_down'].astype(BF16),
    )
    return q


def _pick(n, cap):
    t = min(n, cap)
    assert n % t == 0
    return t


def _layer(x, pos0, past_k, past_v, ssm_h0, conv_hist, pool_hist, p, layer_idx, interpret=False):
    B, L, D = x.shape
    assert L >= POOL_MAX and L % CHUNK == 0
    lam_init = 0.8 - 0.6 * math.exp(-0.3 * layer_idx)
    tm = _pick(L, 512)

    lam_p, nrm_p = p['lam'].astype(F32), p['attn_norm'].astype(F32)
    if past_k is None:
        tb = _pick(tm, 256)
        k, v, qt, kb, vt = _qkv_call(x, p['w_qkv'], pos0, tm, tb, interpret)
        attn_out = _attn_call(qt, kb, vt, lam_p, nrm_p, lam_init, True, tb, interpret)
    else:
        tb = LANES
        n_past = past_k.shape[1]
        assert n_past % tb == 0 and L <= tb
        k, v, qb, kb, vb = _qkv_call(x, p['w_qkv'], pos0, tm, None, interpret)
        zpad = jnp.zeros((B, tb - L, Q_COLS), BF16)
        qt = jnp.concatenate([qb, zpad], axis=1).transpose(0, 2, 1)[:, None]
        kb_all = jnp.concatenate([past_k.reshape(B, n_past, Q_COLS).astype(BF16), kb, zpad], axis=1)
        vb_all = jnp.concatenate([past_v.reshape(B, n_past, ATTN_W).astype(BF16), vb, zpad], axis=1)
        vt = vb_all.reshape(B, n_past // tb + 1, tb, ATTN_W).transpose(0, 1, 3, 2)
        attn_out = _attn_call(qt, kb_all, vt, lam_p, nrm_p, lam_init, False, L, interpret)[:, :L]

    ssd_out, ssm_new, conv_new = _ssd_call(x, p['w_z'], p['w_xbc'], p['w_dt'], p['conv_w'], p['conv_b'], p['dt_bias'],
                                           p['a_log'], p['d_skip'], p['ssd_norm'], ssm_h0, conv_hist,
                                           tm, _pick(L, 128), interpret)

    x1, pool_new = _merge_call(x, attn_out, ssd_out, pool_hist, p['w_pool'], p['pool_w_b'], p['pool_scale'], p['w_g'],
                               p['w_ba'], p['w_bs'], p['w_bp'], p['w_o_b'], p['ln1_g'], p['ln1_b'], pos0, tm, interpret)

    T = B * L
    x2 = _moe_call(x1.reshape(T, D), p['w_router'], p['b_router'], p['w_gate_b'], p['w_up_b'], p['w_down_b'],
                   p['ln2_g'], p['ln2_b'], _pick(T, 1024), interpret)
    return x2.reshape(B, L, D), (k.reshape(B, L, H_A, 2 * DH_A), v.reshape(B, L, H_A, DV_A), ssm_new, conv_new, pool_new)


def kernel(x_prompt, x_sample, cache_k, cache_v, state_ssm, state_conv, state_pool, w_in, conv_w, conv_b, dt_bias,
           a_log, d_skip, ssd_norm, lam, attn_norm, pool_w, pool_scale, w_br_attn, w_br_ssd, w_br_pool, w_o, ln1_g,
           ln1_b, router_group_w, router_group_b, router_expert_w, router_expert_b, w_gate, w_up, w_down, ln2_g, ln2_b):
    bp = x_prompt.shape[0]
    depth = w_in.shape[0]
    zero_h = jnp.zeros((bp, H_S, P_S, N_S), F32)
    zero_conv = jnp.zeros((bp, CONV_W - 1, CONV_DIM), F32)
    zero_pool = jnp.zeros((bp, POOL_MAX - 1, POOL_DIM), F32)
    pos0_sample = cache_k.shape[2]
    hp, hs = x_prompt, x_sample
    st_p, st_s = [], []
    for l in range(depth):
        p = _prep_layer_params({
            'w_in': w_in[l], 'conv_w': conv_w[l], 'conv_b': conv_b[l], 'dt_bias': dt_bias[l], 'a_log': a_log[l],
            'd_skip': d_skip[l], 'ssd_norm': ssd_norm[l], 'lam': lam[l], 'attn_norm': attn_norm[l],
            'pool_w': pool_w[l], 'pool_scale': pool_scale[l], 'w_br_attn': w_br_attn[l], 'w_br_ssd': w_br_ssd[l],
            'w_br_pool': w_br_pool[l], 'w_o': w_o[l], 'ln1_g': ln1_g[l], 'ln1_b': ln1_b[l],
            'rg_w': router_group_w[l], 'rg_b': router_group_b[l], 're_w': router_expert_w[l],
            're_b': router_expert_b[l], 'w_gate': w_gate[l], 'w_up': w_up[l], 'w_down': w_down[l],
            'ln2_g': ln2_g[l], 'ln2_b': ln2_b[l],
        })
        hp, sp = _layer(hp, 0, None, None, zero_h, zero_conv, zero_pool, p, l)
        hs, ss = _layer(hs, pos0_sample, cache_k[l], cache_v[l], state_ssm[l], state_conv[l], state_pool[l], p, l)
        st_p.append(sp)
        st_s.append(ss)
    stack = lambda sts, j: jnp.stack([s[j] for s in sts])
    return (hp, hs) + tuple(stack(st_p, j) for j in range(5)) + tuple(stack(st_s, j) for j in range(5))
```

```python
import functools
import math

import jax
import jax.numpy as jnp
from jax import lax
from jax.experimental import pallas as pl
from jax.experimental.pallas import tpu as pltpu

F32 = jnp.float32
BF16 = jnp.bfloat16
HIGHEST = lax.Precision.HIGHEST

D_MODEL = 1024
CHUNK = 64
H_A = 4
DH_A = 64
DV_A = 2 * DH_A
ROT_DIM = DH_A // 4
ROPE_THETA = 500000.0
ATTN_W = H_A * DV_A
D_INNER = D_MODEL
P_S = 64
H_S = D_INNER // P_S
G_S = 4
R_S = H_S // G_S
N_S = 64
CONV_W = 4
CONV_DIM = D_INNER + 2 * G_S * N_S
POOL_SIZES = (2, 4, 8, 16)
N_POOL = 4
POOL_GW = D_MODEL // 8
POOL_DIM = N_POOL * POOL_GW
POOL_MAX = 16
N_BRANCH = 3
Q_COLS = H_A * 2 * DH_A
OFF_K = Q_COLS
OFF_V = OFF_K + Q_COLS
OFF_Z = OFF_V + H_A * DV_A
OFF_XBC = OFF_Z + D_INNER
OFF_DT = OFF_XBC + CONV_DIM
OFF_POOL = OFF_DT + H_S
OFF_GATE = OFF_POOL + POOL_DIM
N_IN = OFF_GATE + N_BRANCH * D_MODEL
E_GROUPS = 4
E_PER_GROUP = 4
N_EXPERTS = E_GROUPS * E_PER_GROUP
D_EXPERT = D_MODEL // 2
DEPTH_ = 2
ALPHA = (2 * DEPTH_) ** 0.25
LN_EPS = 1e-5
RMS_EPS = 1e-5

LANES = 128
GROUP_W = R_S * P_S
VMEM_LIMIT = 56 * 1024 * 1024
NEG_BIG = -1e30


def _const_spec(shape):
    nd = len(shape)
    return pl.BlockSpec(shape, lambda *_: (0,) * nd)


def _params(sem):
    return pltpu.CompilerParams(dimension_semantics=sem, vmem_limit_bytes=VMEM_LIMIT)


def _dot(a, b):
    return jnp.dot(a, b, preferred_element_type=F32)


def _dot_nt(a, b):
    return lax.dot_general(a, b, (((1,), (1,)), ((), ())), preferred_element_type=F32)


def _dot_tn(a, b):
    return lax.dot_general(a, b, (((0,), (0,)), ((), ())), preferred_element_type=F32)


def _split_bf16(v):
    hi = v.astype(BF16)
    return hi, (v - hi.astype(F32)).astype(BF16)


def _silu(x):
    return x * jax.nn.sigmoid(x)


def _layer_norm(x, g, b):
    mu = jnp.mean(x, axis=-1, keepdims=True)
    xc = x - mu
    var = jnp.mean(xc * xc, axis=-1, keepdims=True)
    return xc * lax.rsqrt(var + LN_EPS) * g + b


def _qkv_kernel(x_ref, w_ref, cos_ref, sa_ref, sb_ref, *rest, tb, n_alias):
    k_ref, v_ref, qb_ref, kb_ref, vb_ref = rest[n_alias:]
    xb = x_ref[...].astype(BF16)
    qkv = _dot(xb, w_ref[...])
    c, sa, sb = cos_ref[...], sa_ref[...], sb_ref[...]

    def rope(t):
        heads = []
        for h in range(H_A):
            th = t[:, h * LANES:(h + 1) * LANES]
            heads.append(th * c + pltpu.roll(th, LANES - ROT_DIM // 2, 1) * sa + pltpu.roll(th, ROT_DIM // 2, 1) * sb)
        return jnp.concatenate(heads, axis=1)

    q = rope(qkv[:, 0:OFF_K]) * (DH_A ** -0.5)
    k = rope(qkv[:, OFF_K:OFF_V])
    v = qkv[:, OFF_V:OFF_Z]
    k_ref[...] = k
    v_ref[...] = v
    kb_ref[...] = k.astype(BF16)
    if tb is None:
        qb_ref[...] = q.astype(BF16)
        vb_ref[...] = v.astype(BF16)
    else:
        for s in range(q.shape[0] // tb):
            qb_ref[s] = q[s * tb:(s + 1) * tb, :].T.astype(BF16)
            vb_ref[s] = v[s * tb:(s + 1) * tb, :].T.astype(BF16)


def _rope_tables(pos0, L):
    half = ROT_DIM // 2
    inv = 1.0 / (ROPE_THETA ** (jnp.arange(half, dtype=F32) / half))
    pos = pos0 + jnp.arange(L, dtype=jnp.int32)
    ang = pos.astype(F32)[:, None] * inv[None, :]
    cos, sin = jnp.cos(ang), jnp.sin(ang)
    rest = DH_A - ROT_DIM
    c64 = jnp.concatenate([cos, cos, jnp.ones((L, rest), F32)], axis=1)
    sa64 = jnp.concatenate([-sin, jnp.zeros((L, DH_A - half), F32)], axis=1)
    sb64 = jnp.concatenate([jnp.zeros((L, half), F32), sin, jnp.zeros((L, rest), F32)], axis=1)
    two = lambda t: jnp.concatenate([t, t], axis=1)
    return two(c64), two(sa64), two(sb64)


def _qkv_call(x, w_qkv, pos0, tm, tb, layer, depth, kv_all, interpret):
    B, L, D = x.shape
    alias_in = [] if kv_all is None else list(kv_all)
    kv_spec = pl.BlockSpec((None, None, tm, Q_COLS), lambda b, i: (layer, b, i, 0))
    cos_t, sa_t, sb_t = _rope_tables(pos0, L)
    tok = lambda w: pl.BlockSpec((None, tm, w), lambda b, i: (b, i, 0))
    tab = pl.BlockSpec((tm, LANES), lambda b, i: (i, 0))
    sds = jax.ShapeDtypeStruct
    if tb is None:
        t_spec, t_shape = tok(Q_COLS), sds((B, L, Q_COLS), BF16)
    else:
        t_spec = pl.BlockSpec((None, tm // tb, Q_COLS, tb), lambda b, i: (b, i, 0, 0))
        t_shape = sds((B, L // tb, Q_COLS, tb), BF16)
    return pl.pallas_call(
        functools.partial(_qkv_kernel, tb=tb, n_alias=len(alias_in)),
        grid=(B, L // tm),
        in_specs=[tok(D), _const_spec((D, OFF_Z)), tab, tab, tab] + [pl.BlockSpec(memory_space=pl.ANY)] * len(alias_in),
        out_specs=[kv_spec, kv_spec, t_spec, tok(Q_COLS), t_spec],
        out_shape=[sds((depth, B, L, Q_COLS), F32)] * 2 + [t_shape, sds((B, L, Q_COLS), BF16), t_shape],
        input_output_aliases={5 + n: n for n in range(len(alias_in))},
        compiler_params=_params(("parallel", "parallel")),
        interpret=interpret,
        name="qkv_rope",
    )(x, w_qkv, cos_t, sa_t, sb_t, *alias_in)


def _attn_kernel(qt_ref, k_ref, vt_ref, lam_ref, nrm_ref, o_ref, *scratch, tb, n_full, causal, n_valid, lam_init):
    qi = pl.program_id(1)
    s_s, p_s, acc_s = scratch[0:H_A], scratch[H_A:2 * H_A], scratch[2 * H_A:3 * H_A]
    sub = lax.broadcasted_iota(jnp.int32, (LANES, tb), 0)
    key = lax.broadcasted_iota(jnp.int32, (tb, 2 * tb), 0)
    if causal:
        qry = lax.broadcasted_iota(jnp.int32, (tb, 2 * tb), 1) & (tb - 1)
        shift = CHUNK.bit_length() - 1
        last_mask = lax.shift_right_logical(key, shift) <= lax.shift_right_logical(qry, shift)
        n_loop = qi
    else:
        last_mask = (key < n_valid) if n_valid < tb else None
        n_loop = n_full

    lv = lam_ref[...]
    lam = (jnp.exp(jnp.sum(lv[0:1] * lv[1:2], axis=-1, keepdims=True))
           - jnp.exp(jnp.sum(lv[2:3] * lv[3:4], axis=-1, keepdims=True)) + lam_init)

    heads = [slice(h * LANES, (h + 1) * LANES) for h in range(H_A)]
    q2t = []
    for hs in heads:
        qt = qt_ref[hs, :]
        zero = jnp.zeros_like(qt)
        q2t.append(jnp.concatenate([jnp.where(sub < DH_A, qt, zero), jnp.where(sub >= DH_A, qt, zero)], axis=1))

    def scores(h, j):
        r0 = j * tb if isinstance(j, int) else pl.multiple_of(j * tb, tb)
        return _dot(k_ref[pl.ds(r0, tb), heads[h]], q2t[h])

    def softmax_step(s, m, l):
        m_new = jnp.maximum(m, jnp.max(s, axis=0, keepdims=True))
        alpha = jnp.exp(m - m_new)
        p = jnp.exp(s - m_new)
        return m_new, alpha * l + jnp.sum(p, axis=0, keepdims=True), alpha, p.astype(BF16)

    def body(j, carry):
        slot = j & 1
        out = []
        for h in range(H_A):
            m, l = carry[h]
            pv_prev = _dot(vt_ref[jnp.maximum(j - 1, 0), heads[h], :], p_s[h][1 - slot])
            m, l, alpha, p = softmax_step(s_s[h][slot], m, l)
            p_s[h][slot] = p
            acc_s[h][...] = alpha * (acc_s[h][...] + pv_prev)
            s_s[h][1 - slot] = scores(h, j + 1)
            out.append((m, l))
        return tuple(out)

    for h in range(H_A):
        s_s[h][0] = scores(h, 0)
        p_s[h][1] = jnp.zeros((tb, 2 * tb), BF16)
        acc_s[h][...] = jnp.zeros((LANES, 2 * tb), F32)
    init = (jnp.full((1, 2 * tb), NEG_BIG, F32), jnp.zeros((1, 2 * tb), F32))
    carry = lax.fori_loop(0, n_loop, body, (init,) * H_A)
    last = n_loop & 1
    for h in range(H_A):
        m, l = carry[h]
        pv_prev = _dot(vt_ref[jnp.maximum(n_loop - 1, 0), heads[h], :], p_s[h][1 - last])
        s_last = s_s[h][last]
        if last_mask is not None:
            s_last = jnp.where(last_mask, s_last, NEG_BIG)
        m, l, alpha, p = softmax_step(s_last, m, l)
        acc = alpha * (acc_s[h][...] + pv_prev) + _dot(vt_ref[n_loop, heads[h], :], p)
        on = acc * (1.0 / l)
        ot = on[:, 0:tb] - lam * on[:, tb:2 * tb]
        ot = ot * lax.rsqrt(jnp.mean(ot * ot, axis=0, keepdims=True) + RMS_EPS) * nrm_ref[...] * (1.0 - lam_init)
        o_ref[:, heads[h]] = ot.T.astype(BF16)


def _attn_call(qt, kb, vt, lam, attn_norm, lam_init, causal, n_valid, interpret):
    B, nq, _, tb = qt.shape
    nk = vt.shape[1]
    Lk = kb.shape[1]
    assert Lk == nk * tb and tb % CHUNK == 0 and (not causal or nq == nk)
    kern = functools.partial(_attn_kernel, tb=tb, n_full=nk - 1, causal=causal, n_valid=n_valid, lam_init=lam_init)
    nrm = jnp.broadcast_to(attn_norm.astype(F32).reshape(DV_A, 1), (DV_A, tb))
    return pl.pallas_call(
        kern,
        grid=(B, nq),
        in_specs=[pl.BlockSpec((None, None, ATTN_W, tb), lambda b, i: (b, i, 0, 0)),
                  pl.BlockSpec((None, Lk, Q_COLS), lambda b, i: (b, 0, 0)),
                  pl.BlockSpec((None, nk, ATTN_W, tb), lambda b, i: (b, 0, 0, 0)),
                  _const_spec((4, DH_A)), _const_spec((DV_A, tb))],
        out_specs=pl.BlockSpec((None, tb, ATTN_W), lambda b, i: (b, i, 0)),
        out_shape=jax.ShapeDtypeStruct((B, nq * tb, ATTN_W), BF16),
        scratch_shapes=([pltpu.VMEM((2, tb, 2 * tb), F32)] * H_A + [pltpu.VMEM((2, tb, 2 * tb), BF16)] * H_A
                        + [pltpu.VMEM((LANES, 2 * tb), F32)] * H_A),
        compiler_params=_params(("parallel", "arbitrary")),
        interpret=interpret,
        name="diff_attn",
    )(qt, kb, vt, lam, nrm)


def _ssd_kernel(x_ref, wz_ref, wxbc_ref, wdt_ref, convw_ref, convb_ref, dtb_ref, alog_ref, dskip_ref, nrm_ref,
                expand_ref, h0_ref, chist_ref,
                y_ref, hout_ref, cnew_ref,
                ext_s, xbc_s, z_s, dt_s, h_s, *, ts, q):
    i = pl.program_id(1)
    hist0 = 8 - (CONV_W - 1)

    @pl.when(i == 0)
    def _():
        h_s[...] = jnp.zeros_like(h_s)
        for g in range(G_S):
            h_s[g, g * N_S:(g + 1) * N_S, :] = h0_ref[g]
        ext_s[hist0:8, :] = chist_ref[...]

    xb = x_ref[...].astype(BF16)
    z_s[...] = _dot(xb, wz_ref[...])
    ext_s[8:8 + ts, :] = _dot(xb, wxbc_ref[...])
    dt_raw = _dot(xb, wdt_ref[...])
    x_dt = dt_raw + dtb_ref[...]
    dt_s[...] = jnp.maximum(x_dt, 0.0) + jnp.log1p(jnp.exp(-jnp.abs(x_dt)))

    acc = convb_ref[...] + ext_s[hist0:hist0 + ts, :] * convw_ref[0:1, :]
    for tap in range(1, CONV_W):
        acc = acc + ext_s[hist0 + tap:hist0 + tap + ts, :] * convw_ref[tap:tap + 1, :]
    xbc_s[...] = _silu(acc)
    hist_new = ext_s[ts + hist0:ts + 8, :]
    cnew_ref[...] = hist_new
    ext_s[hist0:8, :] = hist_new

    a_row = -jnp.exp(alog_ref[...])
    ri = lax.broadcasted_iota(jnp.int32, (q, q), 0)
    ci = lax.broadcasted_iota(jnp.int32, (q, q), 1)
    causal = ri >= ci
    tril = jnp.where(causal, 1.0, 0.0).astype(F32)
    lane_g = lax.shift_right_logical(lax.broadcasted_iota(jnp.int32, (q, GROUP_W), 1),
                                     N_S.bit_length() - 1)

    def chunk(c, _):
        r0 = pl.multiple_of(c * q, q)
        rows = pl.ds(r0, q)
        dtc = dt_s[rows, :]
        acs = jnp.dot(tril, dtc * a_row, precision=HIGHEST, preferred_element_type=F32)
        acs_last = acs[q - 1:q, :]
        stack = jnp.concatenate([dtc * jnp.exp(acs_last - acs), jnp.exp(acs),
                                 jnp.broadcast_to(jnp.exp(acs_last), (8, LANES))], axis=0)
        s_hi, s_lo = _split_bf16(stack)
        ex = _dot(s_hi, expand_ref[...]) + _dot(s_lo, expand_ref[...])
        w_e, ein_e, cd_e = ex[0:q], ex[q:2 * q], ex[2 * q:2 * q + 1]
        xs = xbc_s[rows, 0:D_INNER]
        bm = xbc_s[rows, D_INNER:D_INNER + GROUP_W]
        cm = xbc_s[rows, D_INNER + GROUP_W:CONV_DIM]
        xs_b = xs.astype(BF16)
        xdec_b = (xs * w_e).astype(BF16)
        bm_b = bm.astype(BF16)
        acs_t = acs.T
        dt_t = dtc.T
        zero_bc = jnp.zeros((q, GROUP_W), F32)
        zero_x = jnp.zeros((q, GROUP_W), BF16)
        ys = []
        for g in range(G_S):
            gs = slice(g * GROUP_W, (g + 1) * GROUP_W)
            cm_g = jnp.where(lane_g == g, cm, zero_bc).astype(BF16)
            bm_g = jnp.where(lane_g == g, bm, zero_bc).astype(BF16)
            cb = _dot_nt(cm_g, bm_b)
            xs_g = xs_b[:, gs]
            y_g = _dot(cm_g, h_s[g].astype(BF16)) * ein_e[:, gs]
            for r in range(R_S):
                h = g * R_S + r
                seg = acs[:, h:h + 1] - acs_t[h:h + 1, :]
                lm = jnp.exp(jnp.where(causal, seg, NEG_BIG))
                m_h = (cb * lm * dt_t[h:h + 1, :]).astype(BF16)
                y_g = y_g + _dot(m_h, jnp.where(lane_g == r, xs_g, zero_x))
            ys.append(y_g)
            h_s[g] = h_s[g] * cd_e[:, gs] + _dot_tn(bm_g, xdec_b[:, gs])
        y = jnp.concatenate(ys, axis=1) + dskip_ref[...] * xs
        y = y * _silu(z_s[rows, :])
        y = y * lax.rsqrt(jnp.mean(y * y, axis=-1, keepdims=True) + RMS_EPS) * nrm_ref[...]
        y_ref[rows, :] = y.astype(BF16)
        return 0

    lax.fori_loop(0, ts // q, chunk, 0)
    for g in range(G_S):
        hout_ref[g] = h_s[g, g * N_S:(g + 1) * N_S, :]


def _ssd_call(x, w_z, w_xbc, w_dt, conv_w, conv_b, dt_bias, a_log, d_skip, ssd_norm, h0, conv_hist, ts, q, interpret):
    B, L, D = x.shape
    pad = lambda v: jnp.pad(v.astype(F32), (0, LANES - H_S)).reshape(1, LANES)
    expand = (jnp.arange(LANES)[:, None] == (jnp.arange(D_INNER)[None, :] // P_S)).astype(BF16)
    dskip_e = jnp.repeat(d_skip.astype(F32), P_S).reshape(1, D_INNER)
    h0_t = h0.astype(F32).reshape(B, G_S, R_S, P_S, N_S).transpose(0, 1, 4, 2, 3).reshape(B, G_S, N_S, GROUP_W)
    kern = functools.partial(_ssd_kernel, ts=ts, q=q)
    tok = lambda w: pl.BlockSpec((None, ts, w), lambda b, i: (b, i, 0))
    sds = jax.ShapeDtypeStruct
    y, h_t, conv_new = pl.pallas_call(
        kern,
        grid=(B, L // ts),
        in_specs=[tok(D), _const_spec((D, D_INNER)), _const_spec((D, CONV_DIM)), _const_spec((D, LANES)),
                  _const_spec((CONV_W, CONV_DIM)), _const_spec((1, CONV_DIM)), _const_spec((1, LANES)),
                  _const_spec((1, LANES)), _const_spec((1, D_INNER)), _const_spec((1, D_INNER)),
                  _const_spec((LANES, D_INNER)),
                  pl.BlockSpec((None, G_S, N_S, GROUP_W), lambda b, i: (b, 0, 0, 0)),
                  pl.BlockSpec((None, CONV_W - 1, CONV_DIM), lambda b, i: (b, 0, 0))],
        out_specs=[tok(D_INNER),
                   pl.BlockSpec((None, G_S, N_S, GROUP_W), lambda b, i: (b, 0, 0, 0)),
                   pl.BlockSpec((None, CONV_W - 1, CONV_DIM), lambda b, i: (b, 0, 0))],
        out_shape=[sds((B, L, D_INNER), BF16), sds((B, G_S, N_S, GROUP_W), F32), sds((B, CONV_W - 1, CONV_DIM), F32)],
        scratch_shapes=[pltpu.VMEM((ts + 8, CONV_DIM), F32), pltpu.VMEM((ts, CONV_DIM), F32),
                        pltpu.VMEM((ts, D_INNER), F32), pltpu.VMEM((ts, LANES), F32),
                        pltpu.VMEM((G_S, GROUP_W, GROUP_W), F32)],
        compiler_params=_params(("parallel", "arbitrary")),
        interpret=interpret,
        name="ssd",
    )(x, w_z, w_xbc, w_dt, conv_w.astype(F32), conv_b.astype(F32).reshape(1, CONV_DIM), pad(dt_bias), pad(a_log),
      dskip_e, ssd_norm.astype(F32).reshape(1, D_INNER), expand, h0_t, conv_hist.astype(F32))
    ssm_new = h_t.reshape(B, G_S, N_S, R_S, P_S).transpose(0, 1, 3, 4, 2).reshape(B, H_S, P_S, N_S)
    return y, ssm_new, conv_new


def _merge_kernel(x_ref, a_ref, s_ref, phist_ref, wpool_ref, poolw_ref, pscale_ref, wg_ref, wba_ref, wbs_ref,
                  wbp_ref, wo_ref, g_ref, b_ref, x1_ref, pnew_ref, ext_s, *, tm, pos0):
    i = pl.program_id(1)
    hrows = POOL_MAX - 1

    @pl.when(i == 0)
    def _():
        ext_s[POOL_MAX - hrows:POOL_MAX, :] = phist_ref[...]

    x = x_ref[...]
    xb = x.astype(BF16)
    u = _dot(xb, wpool_ref[...])
    ext_s[POOL_MAX:POOL_MAX + tm, :] = u
    pos = pos0 + i * tm + lax.broadcasted_iota(jnp.int32, (tm, 1), 0)
    outs = []
    for gi, w in enumerate(POOL_SIZES):
        cols = slice(gi * POOL_GW, (gi + 1) * POOL_GW)
        ug = u[:, cols]
        win = ug
        for kk in range(1, w):
            win = win + ext_s[POOL_MAX - kk:POOL_MAX - kk + tm, cols]
        cnt = jnp.minimum(pos + 1, w).astype(F32)
        d = win / cnt - ug
        outs.append(_dot(d.astype(BF16), poolw_ref[gi]) * pscale_ref[:, cols])
    pool_out = jnp.concatenate(outs, axis=1).astype(BF16)
    pnew = ext_s[tm + POOL_MAX - hrows:tm + POOL_MAX, :]
    pnew_ref[...] = pnew
    ext_s[POOL_MAX - hrows:POOL_MAX, :] = pnew

    merged = jax.nn.sigmoid(_dot(xb, wg_ref[:, 0:D_MODEL])) * _dot(a_ref[...], wba_ref[...])
    merged = merged + jax.nn.sigmoid(_dot(xb, wg_ref[:, D_MODEL:2 * D_MODEL])) * _dot(s_ref[...], wbs_ref[...])
    merged = merged + jax.nn.sigmoid(_dot(xb, wg_ref[:, 2 * D_MODEL:3 * D_MODEL])) * _dot(pool_out, wbp_ref[...])
    mix = _dot(merged.astype(BF16), wo_ref[...])
    x1_ref[...] = _layer_norm(ALPHA * x + mix, g_ref[...], b_ref[...])


def _merge_call(x, attn_out, ssd_out, pool_hist, w_pool, pool_w, pool_scale, w_g, w_ba, w_bs, w_bp, w_o, ln_g, ln_b,
                pos0, tm, interpret):
    B, L, D = x.shape
    kern = functools.partial(_merge_kernel, tm=tm, pos0=pos0)
    tok = lambda w: pl.BlockSpec((None, tm, w), lambda b, i: (b, i, 0))
    hist = pl.BlockSpec((None, POOL_MAX - 1, POOL_DIM), lambda b, i: (b, 0, 0))
    sds = jax.ShapeDtypeStruct
    return pl.pallas_call(
        kern,
        grid=(B, L // tm),
        in_specs=[tok(D), tok(ATTN_W), tok(D_INNER), hist,
                  _const_spec((D, POOL_DIM)), _const_spec((N_POOL, POOL_GW, POOL_GW)), _const_spec((1, POOL_DIM)),
                  _const_spec((D, N_BRANCH * D)), _const_spec((ATTN_W, D)), _const_spec((D_INNER, D)),
                  _const_spec((POOL_DIM, D)), _const_spec((D, D)), _const_spec((1, D)), _const_spec((1, D))],
        out_specs=[tok(D), hist],
        out_shape=[sds((B, L, D), F32), sds((B, POOL_MAX - 1, POOL_DIM), F32)],
        scratch_shapes=[pltpu.VMEM((tm + POOL_MAX, POOL_DIM), F32)],
        compiler_params=_params(("parallel", "arbitrary")),
        interpret=interpret,
        name="merge",
    )(x, attn_out, ssd_out, pool_hist.astype(F32), w_pool, pool_w, pool_scale.astype(F32).reshape(1, POOL_DIM),
      w_g, w_ba, w_bs, w_bp, w_o, ln_g.astype(F32).reshape(1, D), ln_b.astype(F32).reshape(1, D))


def _route(logits):
    lane = lax.broadcasted_iota(jnp.int32, logits.shape, 1)
    big = jnp.int32(LANES)
    gl = jnp.where(lane < E_GROUPS, logits, NEG_BIG)
    gmax = jnp.max(gl, axis=-1, keepdims=True)
    g_sel = jnp.min(jnp.where(gl == gmax, lane, big), axis=-1, keepdims=True)
    g_w = 1.0 / jnp.sum(jnp.exp(gl - gmax), axis=-1, keepdims=True)
    lo = E_GROUPS + g_sel * E_PER_GROUP
    in_grp = (lane >= lo) & (lane < lo + E_PER_GROUP)
    el = jnp.where(in_grp, logits, NEG_BIG)
    v1 = jnp.max(el, axis=-1, keepdims=True)
    i1 = jnp.min(jnp.where(el == v1, lane, big), axis=-1, keepdims=True)
    el2 = jnp.where(lane == i1, NEG_BIG, el)
    v2 = jnp.max(el2, axis=-1, keepdims=True)
    i2 = jnp.min(jnp.where(el2 == v2, lane, big), axis=-1, keepdims=True)
    e21 = jnp.exp(v2 - v1)
    w1 = g_w / (1.0 + e21)
    w2 = g_w * e21 / (1.0 + e21)
    return jnp.where(lane == i1, w1, 0.0) + jnp.where(lane == i2, w2, 0.0), g_sel


def _moe_kernel(x_ref, wr_ref, br_ref, wg_ref, wu_ref, wd_ref, g_ref, b_ref, o_ref,
                xs_s, comb_s, p_s, pt_s, cum_s, seg_s, *, tm, sb, rc):
    g = pl.program_id(1)

    @pl.when(g == 0)
    def _():
        x = x_ref[...]
        x_hi, x_lo = _split_bf16(x)
        lg = _dot(x_hi, wr_ref[...])
        logits = lg[:, 0:LANES] + lg[:, LANES:2 * LANES] + _dot(x_lo, wr_ref[:, 0:LANES]) + br_ref[...]
        comb, g_sel = _route(logits)
        lane = lax.broadcasted_iota(jnp.int32, (tm, LANES), 1)
        onehot = jnp.where(lane == g_sel, 1.0, 0.0)
        onehot_b = onehot.astype(BF16)
        for c in range(tm // rc):
            ri = c * rc + lax.broadcasted_iota(jnp.int32, (rc, tm), 0)
            ci = lax.broadcasted_iota(jnp.int32, (rc, tm), 1)
            tri = jnp.where(ci < ri, 1.0, 0.0).astype(BF16)
            cum_s[c * rc:(c + 1) * rc, :] = _dot(tri, onehot_b)
        counts = jnp.sum(onehot, axis=0, keepdims=True)
        lane1 = lax.broadcasted_iota(jnp.int32, (1, LANES), 1)
        off = jnp.zeros((1, LANES), F32)
        start = jnp.float32(0.0)
        for gg in range(E_GROUPS):
            cnt = jnp.sum(jnp.where(lane1 == gg, counts, 0.0))
            seg_s[gg] = start.astype(jnp.int32)
            seg_s[E_GROUPS + gg] = (start + cnt).astype(jnp.int32)
            start = start + cnt
            off = off + jnp.where(lane1 > gg, cnt, 0.0)
        dest = jnp.sum(onehot * (off + cum_s[...]), axis=-1, keepdims=True)
        dest_i = dest.astype(jnp.int32)
        dest_row = jnp.broadcast_to(dest, (tm, LANES)).T[0:1, :].astype(jnp.int32)
        for c in range(tm // rc):
            rows = slice(c * rc, (c + 1) * rc)
            ri = c * rc + lax.broadcasted_iota(jnp.int32, (rc, tm), 0)
            ci = lax.broadcasted_iota(jnp.int32, (rc, tm), 1)
            p_s[rows, :] = jnp.where(ri == dest_row, 1.0, 0.0).astype(BF16)
            pt_s[rows, :] = jnp.where(ci == dest_i[rows, :], 1.0, 0.0).astype(BF16)
        p = p_s[...]
        xs_s[...] = _dot(p, x_hi).astype(BF16)
        c_hi, c_lo = _split_bf16(comb)
        comb_s[...] = _dot(p, c_hi) + _dot(p, c_lo)
        o_ref[...] = jnp.zeros_like(o_ref)

    start = seg_s[g]
    end = seg_s[E_GROUPS + g]
    for blk in range(tm // sb):
        @pl.when((start < (blk + 1) * sb) & (end > blk * sb))
        def _():
            rows = slice(blk * sb, (blk + 1) * sb)
            xb = xs_s[rows, :]
            cw = comb_s[rows, :]
            lane = lax.broadcasted_iota(jnp.int32, (sb, LANES), 1)
            upd = jnp.zeros((sb, D_MODEL), F32)
            for j in range(E_PER_GROUP):
                e_lane = E_GROUPS + g * E_PER_GROUP + j
                w_e = jnp.sum(jnp.where(lane == e_lane, cw, 0.0), axis=-1, keepdims=True)
                h = _silu(_dot(xb, wg_ref[j])) * _dot(xb, wu_ref[j]) * w_e
                upd = upd + _dot(h.astype(BF16), wd_ref[j])
            o_ref[rows, :] += upd

    @pl.when(g == E_GROUPS - 1)
    def _():
        a_hi, a_lo = _split_bf16(o_ref[...])
        pt = pt_s[...]
        ff = _dot(pt, a_hi) + _dot(pt, a_lo)
        o_ref[...] = _layer_norm(ALPHA * x_ref[...] + ff, g_ref[...], b_ref[...])


def _moe_call(x1, w_router, b_router, w_gate, w_up, w_down, ln_g, ln_b, tm, interpret):
    T, D = x1.shape
    tok = pl.BlockSpec((tm, D), lambda i, g: (i, 0))
    sb = _pick(tm, 128)
    kern = functools.partial(_moe_kernel, tm=tm, sb=sb, rc=_pick(tm, 128))
    return pl.pallas_call(
        kern,
        grid=(T // tm, E_GROUPS),
        in_specs=[tok, _const_spec((D, 2 * LANES)), _const_spec((1, LANES)),
                  pl.BlockSpec((E_PER_GROUP, D, D_EXPERT), lambda i, g: (g, 0, 0)),
                  pl.BlockSpec((E_PER_GROUP, D, D_EXPERT), lambda i, g: (g, 0, 0)),
                  pl.BlockSpec((E_PER_GROUP, D_EXPERT, D), lambda i, g: (g, 0, 0)),
                  _const_spec((1, D)), _const_spec((1, D))],
        out_specs=tok,
        out_shape=jax.ShapeDtypeStruct((T, D), F32),
        scratch_shapes=[pltpu.VMEM((tm, D), BF16), pltpu.VMEM((tm, LANES), F32),
                        pltpu.VMEM((tm, tm), BF16), pltpu.VMEM((tm, tm), BF16), pltpu.VMEM((tm, LANES), F32),
                        pltpu.SMEM((2 * E_GROUPS,), jnp.int32)],
        compiler_params=_params(("parallel", "arbitrary")),
        interpret=interpret,
        name="moe",
    )(x1, w_router, b_router, w_gate, w_up, w_down, ln_g.astype(F32).reshape(1, D), ln_b.astype(F32).reshape(1, D))


def _prep_layer_params(p):
    w_in = p['w_in']
    pad_dt = jnp.pad(w_in[:, OFF_DT:OFF_POOL], ((0, 0), (0, LANES - H_S)))
    w_router = jnp.pad(jnp.concatenate([p['rg_w'], p['re_w']], axis=1).astype(F32),
                       ((0, 0), (0, LANES - E_GROUPS - N_EXPERTS)))
    w_router = jnp.concatenate(_split_bf16(w_router), axis=1)
    b_router = jnp.pad(jnp.concatenate([p['rg_b'], p['re_b']]).astype(F32),
                       (0, LANES - E_GROUPS - N_EXPERTS)).reshape(1, LANES)
    q = dict(p)
    q.update(
        w_qkv=w_in[:, 0:OFF_Z].astype(BF16), w_z=w_in[:, OFF_Z:OFF_XBC].astype(BF16),
        w_xbc=w_in[:, OFF_XBC:OFF_DT].astype(BF16), w_dt=pad_dt.astype(BF16),
        w_pool=w_in[:, OFF_POOL:OFF_GATE].astype(BF16), w_g=w_in[:, OFF_GATE:N_IN].astype(BF16),
        pool_w_b=p['pool_w'].astype(BF16), w_ba=p['w_br_attn'].astype(BF16), w_bs=p['w_br_ssd'].astype(BF16),
        w_bp=p['w_br_pool'].astype(BF16), w_o_b=p['w_o'].astype(BF16),
        w_router=w_router, b_router=b_router,
        w_gate_b=p['w_gate'].astype(BF16), w_up_b=p['w_up'].astype(BF16), w_down_b=p['w_down'].astype(BF16),
    )
    return q


def _pick(n, cap):
    t = min(n, cap)
    assert n % t == 0
    return t


def _layer(x, pos0, past_k, past_v, ssm_h0, conv_hist, pool_hist, p, layer_idx, kv_all=None, depth=DEPTH_,
           interpret=False):
    B, L, D = x.shape
    assert L >= POOL_MAX and L % CHUNK == 0
    lam_init = 0.8 - 0.6 * math.exp(-0.3 * layer_idx)
    tm = _pick(L, 512)

    lam_p, nrm_p = p['lam'].astype(F32), p['attn_norm'].astype(F32)
    if past_k is None:
        tb = _pick(tm, 256)
        k, v, qt, kb, vt = _qkv_call(x, p['w_qkv'], pos0, tm, tb, layer_idx, depth, kv_all, interpret)
        attn_out = _attn_call(qt, kb, vt, lam_p, nrm_p, lam_init, True, tb, interpret)
    else:
        tb = LANES
        n_past = past_k.shape[1]
        assert n_past % tb == 0 and L <= tb
        k, v, qb, kb, vb = _qkv_call(x, p['w_qkv'], pos0, tm, None, layer_idx, depth, kv_all, interpret)
        zpad = jnp.zeros((B, tb - L, Q_COLS), BF16)
        qt = jnp.concatenate([qb, zpad], axis=1).transpose(0, 2, 1)[:, None]
        kb_all = jnp.concatenate([past_k.reshape(B, n_past, Q_COLS).astype(BF16), kb, zpad], axis=1)
        vb_all = jnp.concatenate([past_v.reshape(B, n_past, ATTN_W).astype(BF16), vb, zpad], axis=1)
        vt = vb_all.reshape(B, n_past // tb + 1, tb, ATTN_W).transpose(0, 1, 3, 2)
        attn_out = _attn_call(qt, kb_all, vt, lam_p, nrm_p, lam_init, False, L, interpret)[:, :L]

    ssd_out, ssm_new, conv_new = _ssd_call(x, p['w_z'], p['w_xbc'], p['w_dt'], p['conv_w'], p['conv_b'], p['dt_bias'],
                                           p['a_log'], p['d_skip'], p['ssd_norm'], ssm_h0, conv_hist,
                                           tm, _pick(L, 128), interpret)

    x1, pool_new = _merge_call(x, attn_out, ssd_out, pool_hist, p['w_pool'], p['pool_w_b'], p['pool_scale'], p['w_g'],
                               p['w_ba'], p['w_bs'], p['w_bp'], p['w_o_b'], p['ln1_g'], p['ln1_b'], pos0, tm, interpret)

    T = B * L
    x2 = _moe_call(x1.reshape(T, D), p['w_router'], p['b_router'], p['w_gate_b'], p['w_up_b'], p['w_down_b'],
                   p['ln2_g'], p['ln2_b'], _pick(T, 1024), interpret)
    return x2.reshape(B, L, D), (k, v, ssm_new, conv_new, pool_new)


def kernel(x_prompt, x_sample, cache_k, cache_v, state_ssm, state_conv, state_pool, w_in, conv_w, conv_b, dt_bias,
           a_log, d_skip, ssd_norm, lam, attn_norm, pool_w, pool_scale, w_br_attn, w_br_ssd, w_br_pool, w_o, ln1_g,
           ln1_b, router_group_w, router_group_b, router_expert_w, router_expert_b, w_gate, w_up, w_down, ln2_g, ln2_b):
    bp = x_prompt.shape[0]
    depth = w_in.shape[0]
    zero_h = jnp.zeros((bp, H_S, P_S, N_S), F32)
    zero_conv = jnp.zeros((bp, CONV_W - 1, CONV_DIM), F32)
    zero_pool = jnp.zeros((bp, POOL_MAX - 1, POOL_DIM), F32)
    pos0_sample = cache_k.shape[2]
    hp, hs = x_prompt, x_sample
    st_p, st_s = [], []
    kv_p = kv_s = None
    for l in range(depth):
        p = _prep_layer_params({
            'w_in': w_in[l], 'conv_w': conv_w[l], 'conv_b': conv_b[l], 'dt_bias': dt_bias[l], 'a_log': a_log[l],
            'd_skip': d_skip[l], 'ssd_norm': ssd_norm[l], 'lam': lam[l], 'attn_norm': attn_norm[l],
            'pool_w': pool_w[l], 'pool_scale': pool_scale[l], 'w_br_attn': w_br_attn[l], 'w_br_ssd': w_br_ssd[l],
            'w_br_pool': w_br_pool[l], 'w_o': w_o[l], 'ln1_g': ln1_g[l], 'ln1_b': ln1_b[l],
            'rg_w': router_group_w[l], 'rg_b': router_group_b[l], 're_w': router_expert_w[l],
            're_b': router_expert_b[l], 'w_gate': w_gate[l], 'w_up': w_up[l], 'w_down': w_down[l],
            'ln2_g': ln2_g[l], 'ln2_b': ln2_b[l],
        })
        hp, sp = _layer(hp, 0, None, None, zero_h, zero_conv, zero_pool, p, l, kv_p, depth)
        hs, ss = _layer(hs, pos0_sample, cache_k[l], cache_v[l], state_ssm[l], state_conv[l], state_pool[l], p, l,
                        kv_s, depth)
        kv_p, kv_s = sp[0:2], ss[0:2]
        st_p.append(sp)
        st_s.append(ss)

    def outputs(kv, sts):
        heads = tuple(t.reshape(t.shape[0:3] + (H_A, DV_A)) for t in kv)
        return heads + tuple(jnp.stack([s[j] for s in sts]) for j in range(2, 5))

    return (hp, hs) + outputs(kv_p, st_p) + outputs(kv_s, st_s)
```
